```python
import math
import jax
import jax.numpy as jnp
from jax import lax
import numpy as np

D_MODEL = 1024
BATCH = 16
SEQ = 256
DEPTH = 2
DEC_BATCH = 8
DEC_SEQ = 4096
PAST_LEN = 512

GRID_W = 64
N_EVEN = (DEPTH + 1) // 2
N_ODD = DEPTH // 2
EPS = 1e-6
NEG_INF = -1e30
QBLK = 128
ROPE_THETA = 10000.0

D_SSD = D_MODEL
SSD_HEADDIM = 64
SSD_HEADS = D_SSD // SSD_HEADDIM
SSD_GROUPS = 4
SSD_HPG = SSD_HEADS // SSD_GROUPS
SSD_STATE = 128
SSD_CHUNK = 128
D_CONV = 5
CONV_CH = D_SSD + 2 * SSD_GROUPS * SSD_STATE
D_POOL = D_MODEL
POOL_WINDOWS = (2, 4, 8, 16)
POOL_GROUP = D_POOL // len(POOL_WINDOWS)
IN_EVEN = D_SSD + CONV_CH + 2 * SSD_HEADS + 2 * D_POOL
DIFF_HEADS = 8
DIFF_HD = 64
W_DIFF = DIFF_HEADS * 2 * DIFF_HD
WIN_HEADS = 16
WIN_KV = 4
WIN_GROUP = WIN_HEADS // WIN_KV
WIN_HD = 64
WINDOW = 128
WIN_BLK = 128
W_WIN = WIN_HEADS * WIN_HD
IN_ODD = 4 * W_DIFF + 2 * W_WIN + 2 * WIN_KV * WIN_HD

kernel_name = 'hybrid_dit_ssd_pool_diffattn_swa_step'


def rmsnorm(x, g=None):
    xf = x.astype(jnp.float32)
    y = (xf * lax.rsqrt(jnp.mean(xf * xf, axis=-1, keepdims=True) + EPS)).astype(x.dtype)
    return y if g is None else y * g


def split_cols(u, sizes):
    offs = np.cumsum(sizes)[:-1].tolist()
    return jnp.split(u, offs, axis=-1)


def split_blocks(a, axis):
    t = a.shape[axis]
    a = a.reshape(a.shape[:axis] + (t // QBLK, QBLK) + a.shape[axis + 1:])
    return jnp.moveaxis(a, axis, 0)


def merge_blocks(a, axis):
    a = jnp.moveaxis(a, 0, axis)
    return a.reshape(a.shape[:axis] + (a.shape[axis] * a.shape[axis + 1],) + a.shape[axis + 2:])


def ada_mod(cvec, w, b):
    m = (jax.nn.silu(cvec) @ w + b)[:, None, :]
    return jnp.split(m, 3, axis=-1)


def modulated_norm(x, shift, scale):
    return rmsnorm(x) * (1 + scale) + shift


def axial_rope_tables(t_len, d):
    n_rows = t_len // GRID_W
    row = jnp.repeat(jnp.arange(n_rows), GRID_W).astype(jnp.float32)
    col = jnp.tile(jnp.arange(GRID_W), n_rows).astype(jnp.float32)
    nf = d // 4
    inv = ROPE_THETA ** (-jnp.arange(nf, dtype=jnp.float32) / nf)
    ang = jnp.stack([row[:, None] * inv, col[:, None] * inv], axis=1)
    return jnp.cos(ang), jnp.sin(ang)


def apply_axial_rope(x, cos, sin):
    nf = cos.shape[-1]
    shp = x.shape
    xr = x.reshape(shp[:-1] + (2, 2, nf))
    x1, x2 = xr[..., 0, :], xr[..., 1, :]
    c, s = cos.astype(x.dtype), sin.astype(x.dtype)
    out = jnp.stack([x1 * c - x2 * s, x2 * c + x1 * s], axis=-2)
    return out.reshape(shp)


def centred_dwconv(x, w, b):
    pad = D_CONV // 2
    y = lax.conv_general_dilated(x, w[:, None, :], window_strides=(1,), padding=[(pad, pad)],
                                 dimension_numbers=('NWC', 'WIO', 'NWC'),
                                 feature_group_count=x.shape[-1])
    return y + b


def ssd_scan(x, dt, A, Bm, Cm, h0):
    bsz, t_len = x.shape[:2]
    nc = t_len // SSD_CHUNK

    def chunk(a):
        return a.reshape((bsz, nc, SSD_CHUNK) + a.shape[2:])

    a = chunk(dt.astype(jnp.float32) * A.astype(jnp.float32))
    xdt = chunk(x * dt[..., None])
    Bc, Cc = chunk(Bm), chunk(Cm)
    a_cs = jnp.cumsum(a, axis=2)
    a_cs_t = jnp.moveaxis(a_cs, 2, -1)
    seg = a_cs_t[..., :, None] - a_cs_t[..., None, :]
    lower = jnp.tril(jnp.ones((SSD_CHUNK, SSD_CHUNK), dtype=bool))
    l_mat = jnp.exp(jnp.where(lower, seg, NEG_INF))
    cb = jnp.einsum('bclgn,bcsgn->bcgls', Cc, Bc)
    y_diag = jnp.einsum('bcgrls,bcsgrp->bclgrp', cb[:, :, :, None] * l_mat, xdt)
    decay_to_end = jnp.exp(a_cs[:, :, -1:] - a_cs)
    states = jnp.einsum('bclgn,bclgr,bclgrp->bcgrpn', Bc, decay_to_end, xdt)
    chunk_decay = jnp.exp(a_cs[:, :, -1])

    def step(h, inp):
        st, dec = inp
        return h * dec[..., None, None] + st, h

    h_final, h_prev = lax.scan(step, h0.astype(states.dtype),
                               (jnp.moveaxis(states, 1, 0), jnp.moveaxis(chunk_decay, 1, 0)))
    h_prev = jnp.moveaxis(h_prev, 0, 1)
    y_off = jnp.einsum('bclgn,bcgrpn,bclgr->bclgrp', Cc, h_prev, jnp.exp(a_cs))
    y = (y_diag + y_off).reshape(x.shape)
    return y.astype(x.dtype), h_final.astype(x.dtype)


def pool_mixer(xb, pool_w, pool_scale):
    bsz, t_len, _ = xb.shape
    ng = len(POOL_WINDOWS)
    xg = xb.reshape(bsz, t_len, ng, POOL_GROUP).astype(jnp.float32)
    cs = jnp.concatenate([jnp.zeros((bsz, 1, ng, POOL_GROUP), jnp.float32), jnp.cumsum(xg, axis=1)], axis=1)
    win = np.array(POOL_WINDOWS)
    left = win // 2
    right = win - 1 - left
    t = jnp.arange(t_len)
    lo = jnp.clip(t[None, :] - left[:, None], 0, t_len)
    hi = jnp.clip(t[None, :] + right[:, None] + 1, 0, t_len)
    gi = jnp.arange(ng)[:, None]
    s = cs[:, hi, gi] - cs[:, lo, gi]
    mean = s / (hi - lo).astype(jnp.float32)[None, :, :, None]
    pooled = (jnp.moveaxis(mean, 1, 2) - xg).astype(xb.dtype)
    y = jnp.einsum('btgc,gcd->btgd', pooled, pool_w).reshape(bsz, t_len, D_POOL)
    return y * pool_scale


def even_mixer(h, w_in, conv_w, conv_b, A_log, dt_bias, D_skip, norm_g, pool_w, pool_scale, w_out, h0_fwd, h0_bwd):
    bsz, t_len, _ = h.shape
    u = h @ w_in
    z_a, xbc, dt_raw, z_b, x_b = split_cols(u, [D_SSD, CONV_CH, 2 * SSD_HEADS, D_POOL, D_POOL])
    xbc = jax.nn.silu(centred_dwconv(xbc, conv_w, conv_b))
    xs, Bm, Cm = split_cols(xbc, [D_SSD, SSD_GROUPS * SSD_STATE, SSD_GROUPS * SSD_STATE])
    xs = xs.reshape(bsz, t_len, SSD_GROUPS, SSD_HPG, SSD_HEADDIM)
    Bm = Bm.reshape(bsz, t_len, SSD_GROUPS, SSD_STATE)
    Cm = Cm.reshape(bsz, t_len, SSD_GROUPS, SSD_STATE)
    dt = jax.nn.softplus(dt_raw.reshape(bsz, t_len, 2, SSD_GROUPS, SSD_HPG)
                         + dt_bias.reshape(2, SSD_GROUPS, SSD_HPG))
    A = -jnp.exp(A_log.reshape(2, SSD_GROUPS, SSD_HPG))

    def to_grp(s):
        return s.reshape(bsz, SSD_GROUPS, SSD_HPG, SSD_HEADDIM, SSD_STATE)

    def flip(a):
        return jnp.flip(a, axis=1)

    y_f, h_f = ssd_scan(xs, dt[:, :, 0], A[0], Bm, Cm, to_grp(h0_fwd))
    y_b, h_b = ssd_scan(flip(xs), flip(dt[:, :, 1]), A[1], flip(Bm), flip(Cm), to_grp(h0_bwd))
    y = y_f + flip(y_b) + xs * D_skip.reshape(SSD_GROUPS, SSD_HPG, 1)
    y_ssd = rmsnorm(y.reshape(bsz, t_len, D_SSD) * jax.nn.silu(z_a), norm_g)
    y_pool = pool_mixer(x_b, pool_w, pool_scale) * jax.nn.silu(z_b)
    out = jnp.concatenate([y_ssd, y_pool], axis=-1) @ w_out
    return (out,
            h_f.reshape(bsz, SSD_HEADS, SSD_HEADDIM, SSD_STATE),
            h_b.reshape(bsz, SSD_HEADS, SSD_HEADDIM, SSD_STATE))


def odd_project(h, w_in):
    bsz, t_len, _ = h.shape
    u = h @ w_in
    qc, kc, vc, zc, qd, kd, vd, zd = split_cols(
        u, [W_DIFF, W_DIFF, W_DIFF, W_DIFF, W_WIN, WIN_KV * WIN_HD, WIN_KV * WIN_HD, W_WIN])
    qc = qc.reshape(bsz, t_len, DIFF_HEADS, 2, DIFF_HD).transpose(0, 2, 3, 1, 4)
    kc = kc.reshape(bsz, t_len, DIFF_HEADS, 2, DIFF_HD).transpose(0, 2, 3, 1, 4)
    vc = vc.reshape(bsz, t_len, DIFF_HEADS, 2 * DIFF_HD).transpose(0, 2, 1, 3)
    qd = qd.reshape(bsz, t_len, WIN_KV, WIN_GROUP, WIN_HD).transpose(0, 2, 3, 1, 4)
    kd = kd.reshape(bsz, t_len, WIN_KV, WIN_HD).transpose(0, 2, 1, 3)
    vd = vd.reshape(bsz, t_len, WIN_KV, WIN_HD).transpose(0, 2, 1, 3)
    return qc, kc, vc, zc, qd, kd, vd, zd


def diff_lambda(lam_params, layer_idx):
    lam_init = 0.8 - 0.6 * math.exp(-0.3 * layer_idx)
    lq1, lk1, lq2, lk2 = lam_params[0], lam_params[1], lam_params[2], lam_params[3]
    lam = (jnp.exp(jnp.sum(lq1 * lk1).astype(jnp.float32))
           - jnp.exp(jnp.sum(lq2 * lk2).astype(jnp.float32)) + lam_init)
    return lam, lam_init


def diff_attention(q, k, v, lam):
    scale = DIFF_HD ** -0.5

    def block(qb):
        s = jnp.einsum('bhiqd,bhikd->bhiqk', qb, k).astype(jnp.float32) * scale
        p = jax.nn.softmax(s, axis=-1)
        attn = p[:, :, 0] - lam * p[:, :, 1]
        return jnp.einsum('bhqk,bhkv->bhqv', attn.astype(v.dtype), v)

    out = lax.map(block, split_blocks(q, axis=3))
    return merge_blocks(out, axis=2)


def sink_attention(q, k, v, sink):
    scale = WIN_HD ** -0.5
    sk = sink.astype(jnp.float32)[None, :, :, None, None]

    def block(qb):
        s = jnp.einsum('bkgqd,bkld->bkgql', qb, k).astype(jnp.float32) * scale
        logits = jnp.concatenate([s, jnp.broadcast_to(sk, s.shape[:-1] + (1,))], axis=-1)
        p = jax.nn.softmax(logits, axis=-1)[..., :-1]
        return jnp.einsum('bkgql,bkld->bkgqd', p.astype(v.dtype), v)

    out = lax.map(block, split_blocks(q, axis=3))
    return merge_blocks(out, axis=3)


def window_sink_attention(q, k, v, k_ctx, v_ctx, sink):
    t_len = q.shape[3]
    nb = t_len // WIN_BLK
    n_ctx = k_ctx.shape[2]
    scale = WIN_HD ** -0.5
    pad = ((0, 0), (0, 0), (WIN_BLK, WIN_BLK), (0, 0))
    k_pad = jnp.pad(k, pad)
    v_pad = jnp.pad(v, pad)
    sk = sink.astype(jnp.float32)[None, :, :, None, None]

    def block(i):
        start = i * WIN_BLK
        qb = lax.dynamic_slice_in_dim(q, start, WIN_BLK, axis=3)
        kw = lax.dynamic_slice_in_dim(k_pad, start, 3 * WIN_BLK, axis=2)
        vw = lax.dynamic_slice_in_dim(v_pad, start, 3 * WIN_BLK, axis=2)
        qpos = start + jnp.arange(WIN_BLK)
        kpos = start - WIN_BLK + jnp.arange(3 * WIN_BLK)
        valid = ((kpos[None, :] >= 0) & (kpos[None, :] < t_len)
                 & (jnp.abs(qpos[:, None] - kpos[None, :]) <= WINDOW))
        s_w = jnp.einsum('bkgqd,bkwd->bkgqw', qb, kw).astype(jnp.float32) * scale
        s_w = jnp.where(valid, s_w, NEG_INF)
        s_c = jnp.einsum('bkgqd,bkld->bkgql', qb, k_ctx).astype(jnp.float32) * scale
        logits = jnp.concatenate([s_c, s_w, jnp.broadcast_to(sk, s_c.shape[:-1] + (1,))], axis=-1)
        p = jax.nn.softmax(logits, axis=-1).astype(v.dtype)
        return (jnp.einsum('bkgql,bkld->bkgqd', p[..., :n_ctx], v_ctx)
                + jnp.einsum('bkgqw,bkwd->bkgqd', p[..., n_ctx:n_ctx + 3 * WIN_BLK], vw))

    out = lax.map(block, jnp.arange(nb))
    return merge_blocks(out, axis=3)


def odd_output(oc, od, zc, zd, subln_g, lam_init, w_out):
    bsz, t_len = zc.shape[:2]
    oc = rmsnorm(oc, subln_g) * (1 - lam_init)
    oc = oc.transpose(0, 2, 1, 3).reshape(bsz, t_len, W_DIFF)
    od = od.transpose(0, 3, 1, 2, 4).reshape(bsz, t_len, W_WIN)
    return jnp.concatenate([oc * jax.nn.silu(zc), od * jax.nn.silu(zd)], axis=-1) @ w_out


def odd_mixer_context(h, w_in, lam_params, subln_g, sink, w_out, layer_idx):
    bsz, t_len, _ = h.shape
    qc, kc, vc, zc, qd, kd, vd, zd = odd_project(h, w_in)
    lam, lam_init = diff_lambda(lam_params, layer_idx)
    oc = diff_attention(qc, kc, vc, lam)
    od = sink_attention(qd, kd, vd, sink.reshape(WIN_KV, WIN_GROUP))
    out = odd_output(oc, od, zc, zd, subln_g, lam_init, w_out)
    k_cache = kc.transpose(0, 1, 3, 2, 4).reshape(bsz, DIFF_HEADS, t_len, 2 * DIFF_HD)
    return out, k_cache, vc, kd, vd


def odd_mixer_latent(h, w_in, lam_params, subln_g, sink, w_out, ck, cv, wk, wv, layer_idx):
    bsz, t_len, _ = h.shape
    qc, kc, vc, zc, qd, kd, vd, zd = odd_project(h, w_in)
    cos_c, sin_c = axial_rope_tables(t_len, DIFF_HD)
    qc = apply_axial_rope(qc, cos_c, sin_c)
    kc = apply_axial_rope(kc, cos_c, sin_c)
    cos_w, sin_w = axial_rope_tables(t_len, WIN_HD)
    qd = apply_axial_rope(qd, cos_w, sin_w)
    kd = apply_axial_rope(kd, cos_w, sin_w)
    n_ctx = ck.shape[2]
    ck = ck.reshape(bsz, DIFF_HEADS, n_ctx, 2, DIFF_HD).transpose(0, 1, 3, 2, 4)
    k_all = jnp.concatenate([ck, kc], axis=3)
    v_all = jnp.concatenate([cv, vc], axis=2)
    lam, lam_init = diff_lambda(lam_params, layer_idx)
    oc = diff_attention(qc, k_all, v_all, lam)
    od = window_sink_attention(qd, kd, vd, wk, wv, sink.reshape(WIN_KV, WIN_GROUP))
    return odd_output(oc, od, zc, zd, subln_g, lam_init, w_out)


def setup_inputs(seed: int = 0) -> dict:
    key = jax.random.key(seed)
    ks = jax.random.split(key, 32)
    f32 = jnp.float32

    def nrm(k, shape, s):
        return jax.random.normal(k, shape, f32) * s

    inp = {}
    inp['x_prompt'] = nrm(ks[0], (BATCH, SEQ, D_MODEL), 1.0)
    inp['x_sample'] = nrm(ks[1], (DEC_BATCH, DEC_SEQ, D_MODEL), 1.0)
    inp['state_ssd_fwd'] = nrm(ks[2], (DEC_BATCH, N_EVEN, SSD_HEADS, SSD_HEADDIM, SSD_STATE), 0.5)
    inp['state_ssd_bwd'] = nrm(ks[3], (DEC_BATCH, N_EVEN, SSD_HEADS, SSD_HEADDIM, SSD_STATE), 0.5)
    inp['cache_diff_k'] = nrm(ks[4], (DEC_BATCH, N_ODD, DIFF_HEADS, PAST_LEN, 2 * DIFF_HD), 1.0)
    inp['cache_diff_v'] = nrm(ks[5], (DEC_BATCH, N_ODD, DIFF_HEADS, PAST_LEN, 2 * DIFF_HD), 1.0)
    inp['cache_win_k'] = nrm(ks[6], (DEC_BATCH, N_ODD, WIN_KV, PAST_LEN, WIN_HD), 1.0)
    inp['cache_win_v'] = nrm(ks[7], (DEC_BATCH, N_ODD, WIN_KV, PAST_LEN, WIN_HD), 1.0)
    inp['c'] = nrm(ks[8], (DEC_BATCH, D_MODEL), 1.0)
    inp['c_ctx'] = nrm(ks[9], (D_MODEL,), 1.0)
    inp['w_ada'] = nrm(ks[10], (DEPTH, D_MODEL, 3 * D_MODEL), 0.5 * D_MODEL ** -0.5)
    inp['b_ada'] = nrm(ks[11], (DEPTH, 3 * D_MODEL), 0.02)
    inp['ev_w_in'] = nrm(ks[12], (N_EVEN, D_MODEL, IN_EVEN), D_MODEL ** -0.5)
    inp['ev_conv_w'] = nrm(ks[13], (N_EVEN, D_CONV, CONV_CH), D_CONV ** -0.5)
    inp['ev_conv_b'] = nrm(ks[14], (N_EVEN, CONV_CH), 0.02)
    inp['ev_A_log'] = jnp.log(jax.random.uniform(ks[15], (N_EVEN, 2, SSD_HEADS), f32, 1.0, 16.0))
    dt0 = jnp.exp(jax.random.uniform(ks[16], (N_EVEN, 2, SSD_HEADS), f32, math.log(1e-3), math.log(1e-1)))
    inp['ev_dt_bias'] = dt0 + jnp.log(-jnp.expm1(-dt0))
    inp['ev_D'] = 1.0 + nrm(ks[17], (N_EVEN, SSD_HEADS), 0.1)
    inp['ev_norm_g'] = 1.0 + nrm(ks[18], (N_EVEN, D_SSD), 0.05)
    inp['ev_pool_w'] = nrm(ks[19], (N_EVEN, len(POOL_WINDOWS), POOL_GROUP, POOL_GROUP), POOL_GROUP ** -0.5)
    inp['ev_pool_scale'] = 1.0 + nrm(ks[20], (N_EVEN, D_POOL), 0.1)
    inp['ev_w_out'] = nrm(ks[21], (N_EVEN, D_SSD + D_POOL, D_MODEL), (D_SSD + D_POOL) ** -0.5)
    inp['od_w_in'] = nrm(ks[22], (N_ODD, D_MODEL, IN_ODD), D_MODEL ** -0.5)
    inp['od_lambda'] = nrm(ks[23], (N_ODD, 4, DIFF_HD), 0.1)
    inp['od_subln_g'] = 1.0 + nrm(ks[24], (N_ODD, 2 * DIFF_HD), 0.05)
    inp['od_sink'] = nrm(ks[25], (N_ODD, WIN_HEADS), 0.5)
    inp['od_w_out'] = nrm(ks[26], (N_ODD, W_DIFF + W_WIN, D_MODEL), (W_DIFF + W_WIN) ** -0.5)
    inp['final_norm_g'] = 1.0 + nrm(ks[27], (D_MODEL,), 0.05)
    return inp


def reference(x_prompt, x_sample, state_ssd_fwd, state_ssd_bwd, cache_diff_k, cache_diff_v,
              cache_win_k, cache_win_v, c, c_ctx, w_ada, b_ada,
              ev_w_in, ev_conv_w, ev_conv_b, ev_A_log, ev_dt_bias, ev_D, ev_norm_g,
              ev_pool_w, ev_pool_scale, ev_w_out,
              od_w_in, od_lambda, od_subln_g, od_sink, od_w_out, final_norm_g):
    x = x_prompt
    n_req = x.shape[0]
    ssd_f, ssd_b, diff_k, diff_v, win_k, win_v = [], [], [], [], [], []
    for l in range(DEPTH):
        j = l // 2
        shift, scale, gate = ada_mod(c_ctx[None, :], w_ada[l], b_ada[l])
        h = modulated_norm(x, shift, scale)
        if l % 2 == 0:
            zero_state = jnp.zeros((n_req, SSD_HEADS, SSD_HEADDIM, SSD_STATE), x.dtype)
            y, h_f, h_b = even_mixer(h, ev_w_in[j], ev_conv_w[j], ev_conv_b[j], ev_A_log[j], ev_dt_bias[j],
                                     ev_D[j], ev_norm_g[j], ev_pool_w[j], ev_pool_scale[j], ev_w_out[j],
                                     zero_state, zero_state)
            ssd_f.append(h_f)
            ssd_b.append(h_b)
        else:
            y, k_d, v_d, k_w, v_w = odd_mixer_context(h, od_w_in[j], od_lambda[j], od_subln_g[j],
                                                      od_sink[j], od_w_out[j], l)
            diff_k.append(k_d)
            diff_v.append(v_d)
            win_k.append(k_w)
            win_v.append(v_w)
        x = x + gate * y
    y_prompt = rmsnorm(x, final_norm_g)

    x = x_sample
    for l in range(DEPTH):
        j = l // 2
        shift, scale, gate = ada_mod(c, w_ada[l], b_ada[l])
        h = modulated_norm(x, shift, scale)
        if l % 2 == 0:
            y, _, _ = even_mixer(h, ev_w_in[j], ev_conv_w[j], ev_conv_b[j], ev_A_log[j], ev_dt_bias[j],
                                 ev_D[j], ev_norm_g[j], ev_pool_w[j], ev_pool_scale[j], ev_w_out[j],
                                 state_ssd_fwd[:, j], state_ssd_bwd[:, j])
        else:
            y = odd_mixer_latent(h, od_w_in[j], od_lambda[j], od_subln_g[j], od_sink[j], od_w_out[j],
                                 cache_diff_k[:, j], cache_diff_v[:, j], cache_win_k[:, j], cache_win_v[:, j], l)
        x = x + gate * y
    y_sample = rmsnorm(x, final_norm_g)

    new_ssd_fwd = jnp.stack(ssd_f, axis=1)
    new_ssd_bwd = jnp.stack(ssd_b, axis=1)
    new_diff_k = jnp.stack(diff_k, axis=1)
    new_diff_v = jnp.stack(diff_v, axis=1)
    new_win_k = jnp.stack(win_k, axis=1)
    new_win_v = jnp.stack(win_v, axis=1)
    return (y_prompt, y_sample, new_ssd_fwd, new_ssd_bwd, new_diff_k, new_diff_v, new_win_k, new_win_v)
```

```python
import functools
import math

import jax
import jax.numpy as jnp
import numpy as np
from jax import lax
from jax.experimental import pallas as pl
from jax.experimental.pallas import tpu as pltpu

F32 = jnp.float32
BF16 = jnp.bfloat16
HIGHEST = lax.Precision.HIGHEST

D = 1024
EPS = 1e-6
NEG = -1e30
LOG2E = 1.4426950408889634
GRID_W = 64
ROPE_THETA = 10000.0
Q = 128
SSD_HEADS = 16
SSD_P = 64
SSD_N = 128
SSD_GROUPS = 4
D_CONV = 5
CONV_CH = 2048
POOL_WINDOWS = (2, 4, 8, 16)
POOL_GROUP = 256
HALO = 8
DIFF_HEADS = 8
DIFF_HD = 64
WIN_KV = 4
WIN_GROUP = 4
WIN_HD = 64
VMEM_LIMIT = 56 * 1024 * 1024


def _cparams(*sem):
    return pltpu.CompilerParams(dimension_semantics=sem, vmem_limit_bytes=VMEM_LIMIT)


def _dot(a, b, **kw):
    return jnp.dot(a, b, preferred_element_type=F32, **kw)


def _dot_nt(a, b):
    return lax.dot_general(a, b, (((1,), (1,)), ((), ())), preferred_element_type=F32)


def _silu(x):
    return x * jax.nn.sigmoid(x)


def _softplus(x):
    return jnp.maximum(x, 0.0) + jnp.log1p(jnp.exp(-jnp.abs(x)))


def _modnorm(x, mod):
    shift = mod[:, :D]
    scale = mod[:, D:2 * D]
    ms = jnp.mean(x * x, axis=-1, keepdims=True)
    return x * lax.rsqrt(ms + EPS) * (1.0 + scale) + shift


def _const_spec(shape):
    nd = len(shape)
    return pl.BlockSpec(shape, lambda *_: (0,) * nd)


def _ada_kernel(c_ref, w_ref, b_ref, o_ref):
    o_ref[0] = _dot(_silu(c_ref[...]), w_ref[0], precision=HIGHEST) + b_ref[0]


def _ada(cvec, w_ada, b_ada):
    depth, _, n3 = w_ada.shape
    rows = cvec.shape[0]
    tn = 1024
    return pl.pallas_call(
        _ada_kernel,
        grid=(depth, n3 // tn),
        in_specs=[pl.BlockSpec((rows, D), lambda l, j: (0, 0)),
                  pl.BlockSpec((1, D, tn), lambda l, j: (l, 0, j)),
                  pl.BlockSpec((1, 1, tn), lambda l, j: (l, 0, j))],
        out_specs=pl.BlockSpec((1, rows, tn), lambda l, j: (l, 0, j)),
        out_shape=jax.ShapeDtypeStruct((depth, rows, n3), F32),
        compiler_params=_cparams("arbitrary", "arbitrary"),
    )(cvec, w_ada, b_ada.reshape(depth, 1, n3))


EVEN_SPLITS = (1024, 2048, 1024, 1024, 128)


def _even_inproj_kernel(x_ref, mod_ref, w_ref, *out_refs):
    hb = _modnorm(x_ref[0], mod_ref[0]).astype(BF16)
    off = 0
    for o_ref in out_refs:
        wd = o_ref.shape[-1]
        o_ref[0] = _dot(hb, w_ref[:, off:off + wd]).astype(o_ref.dtype)
        off += wd


def _even_inproj(x, mod, w, tm):
    b, t, _ = x.shape
    nmod = mod.shape[0]
    ntot = w.shape[1]
    mod_map = (lambda bi, i: (bi, 0, 0)) if nmod > 1 else (lambda bi, i: (0, 0, 0))
    return pl.pallas_call(
        _even_inproj_kernel,
        grid=(b, t // tm),
        in_specs=[pl.BlockSpec((1, tm, D), lambda bi, i: (bi, i, 0)),
                  pl.BlockSpec((1, 1, 3 * D), mod_map),
                  _const_spec((D, ntot))],
        out_specs=[pl.BlockSpec((1, tm, wd), lambda bi, i: (bi, i, 0)) for wd in EVEN_SPLITS],
        out_shape=[jax.ShapeDtypeStruct((b, t, wd), F32) for wd in EVEN_SPLITS],
        compiler_params=_cparams("parallel", "parallel"),
    )(x, mod, w)


def _fill_halo(ext_ref, main, prev, nxt, has_prev, has_next, rows):
    ext_ref[0:HALO] = jnp.where(has_prev, prev, 0.0)
    ext_ref[HALO:HALO + rows] = main
    ext_ref[HALO + rows:2 * HALO + rows] = jnp.where(has_next, nxt, 0.0)


def _conv_silu(ext_ref, w_ref, b_ref):
    pad = D_CONV // 2
    acc = None
    for k in range(D_CONV):
        term = ext_ref[HALO - pad + k:HALO - pad + k + Q, :] * w_ref[k:k + 1, :]
        acc = term if acc is None else acc + term
    return _silu(acc + b_ref[...])


def _tri_masks():
    row = lax.broadcasted_iota(jnp.int32, (Q, Q), 0)
    col = lax.broadcasted_iota(jnp.int32, (Q, Q), 1)
    return row >= col, col >= row


def _head_expand(vec, lane_base):
    k = lax.broadcasted_iota(jnp.int32, (SSD_N, SSD_HEADS * SSD_P), 0)
    j = lax.broadcasted_iota(jnp.int32, (SSD_N, SSD_HEADS * SSD_P), 1)
    sel = (k == lane_base + j // SSD_P).astype(F32)
    return _dot(jnp.broadcast_to(vec, (8, SSD_N)), sel, precision=HIGHEST)[0:1]


def _pair_cols(arr, base, j, lo):
    return jnp.where(lo, arr[:, base + 2 * j:base + 2 * j + 1], arr[:, base + 2 * j + 1:base + 2 * j + 2])


def _load_state_t(h0_ref, ht_ref):
    for j in range(SSD_HEADS // 2):
        pair = jnp.concatenate([h0_ref[0, 2 * j], h0_ref[0, 2 * j + 1]], axis=0)
        ht_ref[:, j * 128:(j + 1) * 128] = pair.T


def _store_state(ht_ref, out_ref):
    for j in range(SSD_HEADS // 2):
        tr = ht_ref[:, j * 128:(j + 1) * 128].T
        out_ref[0, 2 * j] = tr[0:SSD_P]
        out_ref[0, 2 * j + 1] = tr[SSD_P:2 * SSD_P]


def _ssd_bwd_kernel(*refs, nc, has_h0, want_final):
    it = iter(refs)
    xm_ref, xp_ref, xn_ref, dt_ref, cw_ref, cb_ref, prm_ref = (next(it) for _ in range(7))
    h0_ref = next(it) if has_h0 else None
    hprev_ref = next(it)
    hfin_ref = next(it) if want_final else None
    ht_ref, ext_ref = next(it), next(it)

    i = pl.program_id(1)
    c = nc - 1 - i

    @pl.when(i == 0)
    def _():
        if has_h0:
            _load_state_t(h0_ref, ht_ref)
        else:
            ht_ref[...] = jnp.zeros_like(ht_ref)

    _fill_halo(ext_ref, xm_ref[0], xp_ref[0], xn_ref[0], c > 0, c < nc - 1, Q)
    act = _conv_silu(ext_ref, cw_ref, cb_ref)

    dt = _softplus(dt_ref[0] + prm_ref[0:1, :])
    a = dt * (-jnp.exp(prm_ref[1:2, :]))
    _, upper = _tri_masks()
    acs = _dot(upper.astype(F32), a, precision=HIGHEST)
    wgt = dt * jnp.exp(acs[0:1, :] - acs)
    cd_exp = _head_expand(jnp.exp(acs[0:1, :]), SSD_HEADS)

    lo = lax.broadcasted_iota(jnp.int32, (Q, 128), 1) < SSD_P
    hprev_ref[0, 0] = ht_ref[...]
    for g in range(SSD_GROUPS):
        xw = jnp.concatenate(
            [(act[:, j * 128:(j + 1) * 128] * _pair_cols(wgt, SSD_HEADS, j, lo)).astype(BF16)
             for j in (2 * g, 2 * g + 1)], axis=1)
        bt = act[:, D + g * SSD_N:D + (g + 1) * SSD_N].T.astype(BF16)
        sl = slice(g * 256, (g + 1) * 256)
        ht_ref[:, sl] = ht_ref[:, sl] * cd_exp[:, sl] + _dot(bt, xw)

    if want_final:
        @pl.when(i == nc - 1)
        def _():
            _store_state(ht_ref, hfin_ref)


def _ssd_main_kernel(*refs, nc, has_h0, want_final):
    it = iter(refs)
    (xm_ref, xp_ref, xn_ref, dt_ref, za_ref, hpb_ref, cw_ref, cb_ref, prm_ref, prmt_ref, dsk_ref,
     ng_ref) = (next(it) for _ in range(12))
    h0_ref = next(it) if has_h0 else None
    y_ref = next(it)
    hfin_ref = next(it) if want_final else None
    ht_ref, ext_ref, ybuf_ref = next(it), next(it), next(it)

    c = pl.program_id(1)

    @pl.when(c == 0)
    def _():
        if has_h0:
            _load_state_t(h0_ref, ht_ref)
        else:
            ht_ref[...] = jnp.zeros_like(ht_ref)

    _fill_halo(ext_ref, xm_ref[0], xp_ref[0], xn_ref[0], c > 0, c < nc - 1, Q)
    act = _conv_silu(ext_ref, cw_ref, cb_ref)
    lower, upper = _tri_masks()
    lower_f, upper_f = lower.astype(F32), upper.astype(F32)

    dtr = dt_ref[0]
    dt_c = _softplus(dtr + prm_ref[0:1, :])
    a_c = dt_c * (-jnp.exp(prm_ref[1:2, :]))
    acsf_c = _dot(lower_f, a_c, precision=HIGHEST)
    acsb_c = _dot(upper_f, a_c, precision=HIGHEST)
    ef_c = jnp.exp(acsf_c)
    eb_c = jnp.exp(acsb_c)
    wf_c = dt_c * jnp.exp(acsf_c[Q - 1:Q, :] - acsf_c)
    cdf_exp = _head_expand(jnp.exp(acsf_c[Q - 1:Q, :]), 0)

    nh2 = 2 * SSD_HEADS
    dt_r = _softplus(dtr.T[0:nh2] + prmt_ref[0])
    a_r = dt_r * (-jnp.exp(prmt_ref[1]))
    acsf_r = _dot(a_r, upper_f, precision=HIGHEST)
    acsb_r = _dot(a_r, lower_f, precision=HIGHEST)

    lo = lax.broadcasted_iota(jnp.int32, (Q, 128), 1) < SSD_P
    hpb = hpb_ref[0, 0]
    for g in range(SSD_GROUPS):
        sl = slice(g * 256, (g + 1) * 256)
        cg = act[:, D + 512 + g * SSD_N:D + 512 + (g + 1) * SSD_N].astype(BF16)
        bt = act[:, D + g * SSD_N:D + (g + 1) * SSD_N].T.astype(BF16)
        cbm = _dot(cg, bt)
        hf_old = ht_ref[:, sl]
        yof = _dot(cg, hf_old.astype(BF16))
        yob = _dot(cg, hpb[:, sl].astype(BF16))
        xw = []
        for jj in range(2):
            j = 2 * g + jj
            xs_pair = act[:, j * 128:(j + 1) * 128]
            xs_pair_b = xs_pair.astype(BF16)
            yd = []
            for hh in (2 * j, 2 * j + 1):
                hb_ = SSD_HEADS + hh
                lf = jnp.exp(jnp.where(lower, acsf_c[:, hh:hh + 1] - acsf_r[hh:hh + 1, :], NEG))
                lb = jnp.exp(jnp.where(upper, acsb_c[:, hb_:hb_ + 1] - acsb_r[hb_:hb_ + 1, :], NEG))
                wmat = cbm * (lf * dt_r[hh:hh + 1, :] + lb * dt_r[hb_:hb_ + 1, :])
                yd.append(_dot(wmat.astype(BF16), xs_pair_b))
            y_pair = (jnp.where(lo, yd[0], yd[1])
                      + yof[:, jj * 128:(jj + 1) * 128] * _pair_cols(ef_c, 0, j, lo)
                      + yob[:, jj * 128:(jj + 1) * 128] * _pair_cols(eb_c, SSD_HEADS, j, lo)
                      + xs_pair * dsk_ref[:, j * 128:(j + 1) * 128])
            ybuf_ref[:, j * 128:(j + 1) * 128] = y_pair
            xw.append((xs_pair * _pair_cols(wf_c, 0, j, lo)).astype(BF16))
        ht_ref[:, sl] = hf_old * cdf_exp[:, sl] + _dot(bt, jnp.concatenate(xw, axis=1))

    yz = ybuf_ref[...] * _silu(za_ref[0])
    ms = jnp.mean(yz * yz, axis=-1, keepdims=True)
    y_ref[0] = (yz * lax.rsqrt(ms + EPS) * ng_ref[...]).astype(y_ref.dtype)

    if want_final:
        @pl.when(c == nc - 1)
        def _():
            _store_state(ht_ref, hfin_ref)


def _halo_specs(width, nc, rev):
    rpc = Q // HALO
    nblk = nc * rpc
    cidx = (lambda i: nc - 1 - i) if rev else (lambda i: i)
    return [pl.BlockSpec((1, Q, width), lambda b, i: (b, cidx(i), 0)),
            pl.BlockSpec((1, HALO, width), lambda b, i: (b, jnp.maximum(cidx(i) * rpc - 1, 0), 0)),
            pl.BlockSpec((1, HALO, width), lambda b, i: (b, jnp.minimum((cidx(i) + 1) * rpc, nblk - 1), 0))]


def _ssd(xbc, dt, za, conv_w, conv_b, prm, prmt, dsk, ng, h0f, h0b, want_final):
    b, t, _ = xbc.shape
    nc = t // Q
    has_h0 = h0f is not None
    hw = SSD_HEADS * SSD_P
    state_shape = jax.ShapeDtypeStruct((b, SSD_HEADS, SSD_P, SSD_N), F32)
    state_spec = pl.BlockSpec((1, SSD_HEADS, SSD_P, SSD_N), lambda bi, i: (bi, 0, 0, 0))
    wb = D + SSD_GROUPS * SSD_N

    in_specs = (_halo_specs(wb, nc, True)
                + [pl.BlockSpec((1, Q, 128), lambda bi, i: (bi, nc - 1 - i, 0)),
                   _const_spec((D_CONV, wb)), _const_spec((1, wb)), _const_spec((8, 128))])
    args = [xbc, xbc, xbc, dt, conv_w[:, :wb], conv_b[:, :wb], prm]
    if has_h0:
        in_specs.append(state_spec)
        args.append(h0b)
    out_specs = [pl.BlockSpec((1, 1, SSD_N, hw), lambda bi, i: (bi, nc - 1 - i, 0, 0))]
    out_shape = [jax.ShapeDtypeStruct((b, nc, SSD_N, hw), F32)]
    if want_final:
        out_specs.append(state_spec)
        out_shape.append(state_shape)
    res = pl.pallas_call(
        functools.partial(_ssd_bwd_kernel, nc=nc, has_h0=has_h0, want_final=want_final),
        grid=(b, nc), in_specs=in_specs, out_specs=out_specs, out_shape=out_shape,
        scratch_shapes=[pltpu.VMEM((SSD_N, hw), F32), pltpu.VMEM((Q + 2 * HALO, wb), F32)],
        compiler_params=_cparams("parallel", "arbitrary"),
    )(*args)
    hprev_b = res[0]
    hfin_b = res[1] if want_final else None

    in_specs = (_halo_specs(CONV_CH, nc, False)
                + [pl.BlockSpec((1, Q, 128), lambda bi, i: (bi, i, 0)),
                   pl.BlockSpec((1, Q, hw), lambda bi, i: (bi, i, 0)),
                   pl.BlockSpec((1, 1, SSD_N, hw), lambda bi, i: (bi, i, 0, 0)),
                   _const_spec((D_CONV, CONV_CH)), _const_spec((1, CONV_CH)), _const_spec((8, 128)),
                   _const_spec((2, 2 * SSD_HEADS, 128)), _const_spec((1, hw)), _const_spec((1, hw))])
    args = [xbc, xbc, xbc, dt, za, hprev_b, conv_w, conv_b, prm, prmt, dsk, ng]
    if has_h0:
        in_specs.append(state_spec)
        args.append(h0f)
    out_specs = [pl.BlockSpec((1, Q, hw), lambda bi, i: (bi, i, 0))]
    out_shape = [jax.ShapeDtypeStruct((b, t, hw), BF16)]
    if want_final:
        out_specs.append(state_spec)
        out_shape.append(state_shape)
    res = pl.pallas_call(
        functools.partial(_ssd_main_kernel, nc=nc, has_h0=has_h0, want_final=want_final),
        grid=(b, nc), in_specs=in_specs, out_specs=out_specs, out_shape=out_shape,
        scratch_shapes=[pltpu.VMEM((SSD_N, hw), F32), pltpu.VMEM((Q + 2 * HALO, CONV_CH), F32),
                        pltpu.VMEM((Q, hw), F32)],
        compiler_params=_cparams("parallel", "arbitrary"),
    )(*args)
    return res[0], (res[1] if want_final else None), hfin_b


def _even_out_kernel(y_ref, xm_ref, xp_ref, xn_ref, zb_ref, x_ref, mod_ref, pw_ref, ps_ref, wo_ref, o_ref,
                     ext_ref, *, t_len, tm, nt):
    i = pl.program_id(1)
    _fill_halo(ext_ref, xm_ref[0], xp_ref[0], xn_ref[0], i > 0, i < nt - 1, tm)
    tpos = i * tm + lax.broadcasted_iota(jnp.int32, (tm, POOL_GROUP), 0)
    parts = []
    for g, w in enumerate(POOL_WINDOWS):
        left = w // 2
        right = w - 1 - left
        cols = slice(g * POOL_GROUP, (g + 1) * POOL_GROUP)
        s = None
        for k in range(-left, right + 1):
            v = ext_ref[HALO + k:HALO + k + tm, cols]
            s = v if s is None else s + v
        cnt = (jnp.minimum(tpos + right + 1, t_len) - jnp.maximum(tpos - left, 0)).astype(F32)
        pooled = s / cnt - ext_ref[HALO:HALO + tm, cols]
        parts.append(_dot(pooled.astype(BF16), pw_ref[g]))
    yp = jnp.concatenate(parts, axis=1) * ps_ref[...] * _silu(zb_ref[0])
    out = _dot(y_ref[0], wo_ref[0:D, :]) + _dot(yp.astype(BF16), wo_ref[D:2 * D, :])
    o_ref[0] = x_ref[0] + mod_ref[0][:, 2 * D:3 * D] * out


def _even_out(y, xb, zb, x, mod, pool_w, pool_scale, w_out, tm):
    b, t, _ = x.shape
    nt = t // tm
    rpt = tm // HALO
    nblk = t // HALO
    nmod = mod.shape[0]
    mod_map = (lambda bi, i: (bi, 0, 0)) if nmod > 1 else (lambda bi, i: (0, 0, 0))
    tile = pl.BlockSpec((1, tm, D), lambda bi, i: (bi, i, 0))
    return pl.pallas_call(
        functools.partial(_even_out_kernel, t_len=t, tm=tm, nt=nt),
        grid=(b, nt),
        in_specs=[tile, tile,
                  pl.BlockSpec((1, HALO, D), lambda bi, i: (bi, jnp.maximum(i * rpt - 1, 0), 0)),
                  pl.BlockSpec((1, HALO, D), lambda bi, i: (bi, jnp.minimum((i + 1) * rpt, nblk - 1), 0)),
                  tile, tile, pl.BlockSpec((1, 1, 3 * D), mod_map),
                  _const_spec(pool_w.shape), _const_spec((1, D)), _const_spec((2 * D, D))],
        out_specs=tile,
        out_shape=jax.ShapeDtypeStruct((b, t, D), F32),
        scratch_shapes=[pltpu.VMEM((tm + 2 * HALO, D), F32)],
        compiler_params=_cparams("parallel", "parallel"),
    )(y, xb, xb, xb, zb, x, mod, pool_w, pool_scale, w_out)


W_DIFF = DIFF_HEADS * 2 * DIFF_HD
W_WIN = WIN_KV * WIN_GROUP * WIN_HD
W_KVW = WIN_KV * WIN_HD
Q_SCALE = DIFF_HD ** -0.5 * LOG2E


def _rope(u, cos, sin, first):
    outs = []
    for j in range(u.shape[1] // 128):
        x = u[:, j * 128:(j + 1) * 128]
        partner = jnp.where(first, pltpu.roll(x, 112, 1), pltpu.roll(x, 16, 1))
        outs.append(x * cos + partner * sin)
    return outs[0] if len(outs) == 1 else jnp.concatenate(outs, axis=1)


def _odd_inproj_kernel(*refs, latent):
    it = iter(refs)
    x_ref, mod_ref, w_ref = next(it), next(it), next(it)
    if latent:
        cos_ref, sin_ref = next(it), next(it)
    qc_ref, kc_ref, vc_ref, zc_ref, qd_ref, kd_ref, vd_ref, zd_ref = (next(it) for _ in range(8))

    hb = _modnorm(x_ref[0], mod_ref[0]).astype(BF16)
    if latent:
        cos, sin = cos_ref[...], sin_ref[...]
        first = (lax.broadcasted_iota(jnp.int32, cos.shape, 1) % 32) < 16
        rope = lambda u: _rope(u, cos, sin, first)
    else:
        rope = lambda u: u

    def proj(off, width):
        return _dot(hb, w_ref[:, off:off + width])

    qc_ref[0] = (rope(proj(0, W_DIFF)) * Q_SCALE).astype(qc_ref.dtype)
    kc = rope(proj(W_DIFF, W_DIFF))
    vc = proj(2 * W_DIFF, W_DIFF)
    if latent:
        kc_ref[0] = kc.astype(kc_ref.dtype)
        vc_ref[0] = vc.astype(vc_ref.dtype)
    else:
        for h in range(DIFF_HEADS):
            kc_ref[0, h] = kc[:, h * 128:(h + 1) * 128]
            vc_ref[0, h] = vc[:, h * 128:(h + 1) * 128]
    zc_ref[0] = proj(3 * W_DIFF, W_DIFF)
    off = 4 * W_DIFF
    qd_ref[0] = (rope(proj(off, W_WIN)) * Q_SCALE).astype(qd_ref.dtype)
    kd = rope(proj(off + W_WIN, W_KVW))
    vd = proj(off + W_WIN + W_KVW, W_KVW)
    for kv in range(WIN_KV):
        kd_ref[0, kv] = kd[:, kv * WIN_HD:(kv + 1) * WIN_HD].astype(kd_ref.dtype)
        vd_ref[0, kv] = vd[:, kv * WIN_HD:(kv + 1) * WIN_HD].astype(vd_ref.dtype)
    zd_ref[0] = proj(off + W_WIN + 2 * W_KVW, W_WIN)


def _odd_inproj(x, mod, w, tm, rope_tabs):
    b, t, _ = x.shape
    latent = rope_tabs is not None
    nmod = mod.shape[0]
    mod_map = (lambda bi, i: (bi, 0, 0)) if nmod > 1 else (lambda bi, i: (0, 0, 0))
    tile = lambda wd: pl.BlockSpec((1, tm, wd), lambda bi, i: (bi, i, 0))
    hm = lambda nh, hd: pl.BlockSpec((1, nh, tm, hd), lambda bi, i: (bi, 0, i, 0))
    in_specs = [tile(D), pl.BlockSpec((1, 1, 3 * D), mod_map), _const_spec(w.shape)]
    args = [x, mod, w]
    if latent:
        in_specs += [pl.BlockSpec((tm, 128), lambda bi, i: (i, 0))] * 2
        args += list(rope_tabs)
        kv_specs = [tile(W_DIFF), tile(W_DIFF)]
        kv_shapes = [jax.ShapeDtypeStruct((b, t, W_DIFF), BF16)] * 2
        wdt = BF16
    else:
        kv_specs = [hm(DIFF_HEADS, 128), hm(DIFF_HEADS, 128)]
        kv_shapes = [jax.ShapeDtypeStruct((b, DIFF_HEADS, t, 128), F32)] * 2
        wdt = F32
    out_specs = ([tile(W_DIFF)] + kv_specs + [tile(W_DIFF), tile(W_WIN), hm(WIN_KV, WIN_HD), hm(WIN_KV, WIN_HD),
                                              tile(W_WIN)])
    out_shape = ([jax.ShapeDtypeStruct((b, t, W_DIFF), BF16)] + kv_shapes
                 + [jax.ShapeDtypeStruct((b, t, W_DIFF), F32), jax.ShapeDtypeStruct((b, t, W_WIN), BF16),
                    jax.ShapeDtypeStruct((b, WIN_KV, t, WIN_HD), wdt),
                    jax.ShapeDtypeStruct((b, WIN_KV, t, WIN_HD), wdt),
                    jax.ShapeDtypeStruct((b, t, W_WIN), F32)])
    return pl.pallas_call(
        functools.partial(_odd_inproj_kernel, latent=latent),
        grid=(b, t // tm), in_specs=in_specs, out_specs=out_specs, out_shape=out_shape,
        compiler_params=_cparams("parallel", "parallel"),
    )(*args)


def _diff_attn_kernel(*refs, n_ctx, head_major, lam_init):
    it = iter(refs)
    q_ref, kl_ref, vl_ref = next(it), next(it), next(it)
    if n_ctx:
        kc_ref, vc_ref = next(it), next(it)
    z_ref, lam_ref, g_ref, o_ref, kall_ref, vext_ref = (next(it) for _ in range(6))

    @pl.when(pl.program_id(2) == 0)
    def _():
        kl = kl_ref[0, 0] if head_major else kl_ref[0]
        vl = vl_ref[0, 0] if head_major else vl_ref[0]
        if n_ctx:
            kall_ref[0:n_ctx] = kc_ref[0, 0].astype(BF16)
            vext_ref[0:n_ctx, 0:128] = vc_ref[0, 0].astype(BF16)
        kall_ref[n_ctx:] = kl.astype(BF16)
        vext_ref[n_ctx:, 0:128] = vl.astype(BF16)
        vext_ref[:, 128:256] = jnp.ones((vext_ref.shape[0], 128), BF16)

    q = q_ref[0]
    lo = lax.broadcasted_iota(jnp.int32, q.shape, 1) < DIFF_HD
    lp = lam_ref[...]
    lam = (jnp.exp(jnp.sum(lp[0:1] * lp[1:2], axis=-1, keepdims=True))
           - jnp.exp(jnp.sum(lp[2:3] * lp[3:4], axis=-1, keepdims=True)) + lam_init)
    outs = []
    for idx in range(2):
        qp = jnp.where(lo if idx == 0 else jnp.logical_not(lo), q, jnp.zeros_like(q))
        s = _dot_nt(qp, kall_ref[...])
        m = jnp.max(s, axis=-1, keepdims=True)
        p = jnp.exp2(s - m).astype(BF16)
        oe = _dot(p, vext_ref[...])
        outs.append(oe[:, 0:128] / oe[:, 128:129])
    o = outs[0] - lam * outs[1]
    ms = jnp.mean(o * o, axis=-1, keepdims=True)
    on = o * lax.rsqrt(ms + EPS) * g_ref[...] * (1.0 - lam_init)
    o_ref[0] = (on * _silu(z_ref[0])).astype(o_ref.dtype)


def _diff_attn(q, k, v, kc, vc, z, lam_p, subln_g, tq, lam_init):
    b, t, _ = q.shape
    head_major = k.ndim == 4
    n_ctx = 0 if kc is None else kc.shape[2]
    nk = n_ctx + t
    qspec = pl.BlockSpec((1, tq, 128), lambda bi, h, i: (bi, i, h))
    if head_major:
        kvspec = pl.BlockSpec((1, 1, t, 128), lambda bi, h, i: (bi, h, 0, 0))
    else:
        kvspec = pl.BlockSpec((1, t, 128), lambda bi, h, i: (bi, 0, h))
    in_specs = [qspec, kvspec, kvspec]
    args = [q, k, v]
    if n_ctx:
        cspec = pl.BlockSpec((1, 1, n_ctx, 128), lambda bi, h, i: (bi, h, 0, 0))
        in_specs += [cspec, cspec]
        args += [kc, vc]
    in_specs += [qspec, _const_spec((4, DIFF_HD)), _const_spec((1, 128))]
    args += [z, lam_p, subln_g]
    return pl.pallas_call(
        functools.partial(_diff_attn_kernel, n_ctx=n_ctx, head_major=head_major, lam_init=lam_init),
        grid=(b, DIFF_HEADS, t // tq), in_specs=in_specs, out_specs=qspec,
        out_shape=jax.ShapeDtypeStruct((b, t, W_DIFF), BF16),
        scratch_shapes=[pltpu.VMEM((nk, 128), BF16), pltpu.VMEM((nk, 256), BF16)],
        compiler_params=_cparams("parallel", "parallel", "arbitrary"),
    )(*args)


def _sink_attn_kernel(*refs, n_ctx, window, nb):
    it = iter(refs)
    q_ref = next(it)
    if window:
        kp_ref, kcur_ref, kn_ref, vp_ref, vcur_ref, vn_ref = (next(it) for _ in range(6))
    else:
        kcur_ref, vcur_ref = next(it), next(it)
    if n_ctx:
        kc_ref, vc_ref = next(it), next(it)
    z_ref, sink_ref, o_ref = next(it), next(it), next(it)

    kvh = pl.program_id(1)
    i = pl.program_id(2)
    qb = q_ref[0]
    tq = qb.shape[0]
    if window:
        kw = jnp.concatenate([kp_ref[0, 0], kcur_ref[0, 0], kn_ref[0, 0]], axis=0).astype(BF16)
        vw = jnp.concatenate([vp_ref[0, 0], vcur_ref[0, 0], vn_ref[0, 0]], axis=0).astype(BF16)
        r = lax.broadcasted_iota(jnp.int32, (tq, 3 * Q), 0)
        j = lax.broadcasted_iota(jnp.int32, (tq, 3 * Q), 1)
        valid = (j >= r) & (j <= r + 2 * Q) & ((j >= Q) | (i > 0)) & ((j < 2 * Q) | (i < nb - 1))
    else:
        kw = kcur_ref[0, 0].astype(BF16)
        vw = vcur_ref[0, 0].astype(BF16)
    if n_ctx:
        kc = kc_ref[0, 0].astype(BF16)
        vc = vc_ref[0, 0].astype(BF16)
    zb = z_ref[0]
    outs = []
    for g in range(WIN_GROUP):
        qg = qb[:, g * WIN_HD:(g + 1) * WIN_HD]
        sk = sink_ref[pl.ds(kvh * WIN_GROUP + g, 1), :][:, 0:1] * LOG2E
        sw = _dot_nt(qg, kw)
        if window:
            sw = jnp.where(valid, sw, NEG)
        m = jnp.maximum(jnp.max(sw, axis=-1, keepdims=True), sk)
        if n_ctx:
            sc = _dot_nt(qg, kc)
            m = jnp.maximum(m, jnp.max(sc, axis=-1, keepdims=True))
        pw = jnp.exp2(sw - m)
        den = jnp.sum(pw, axis=-1, keepdims=True) + jnp.exp2(sk - m)
        acc = _dot(pw.astype(BF16), vw)
        if n_ctx:
            pc = jnp.exp2(sc - m)
            den = den + jnp.sum(pc, axis=-1, keepdims=True)
            acc = acc + _dot(pc.astype(BF16), vc)
        outs.append(acc / den * _silu(zb[:, g * WIN_HD:(g + 1) * WIN_HD]))
    o_ref[0] = jnp.concatenate(outs, axis=1).astype(o_ref.dtype)


def _sink_attn(q, k, v, kc, vc, z, sink_b, window):
    b, t, _ = q.shape
    n_ctx = 0 if kc is None else kc.shape[2]
    gw = WIN_GROUP * WIN_HD
    if window:
        tq, nb = Q, t // Q
        blk = lambda f: pl.BlockSpec((1, 1, Q, WIN_HD), lambda bi, kv, i: (bi, kv, f(i), 0))
        kv_specs = [blk(lambda i: jnp.maximum(i - 1, 0)), blk(lambda i: i), blk(lambda i: jnp.minimum(i + 1, nb - 1))]
        in_kv = kv_specs + kv_specs
        args_kv = [k, k, k, v, v, v]
    else:
        tq, nb = t, 1
        in_kv = [pl.BlockSpec((1, 1, t, WIN_HD), lambda bi, kv, i: (bi, kv, 0, 0))] * 2
        args_kv = [k, v]
    qspec = pl.BlockSpec((1, tq, gw), lambda bi, kv, i: (bi, i, kv))
    in_specs = [qspec] + in_kv
    args = [q] + args_kv
    if n_ctx:
        cspec = pl.BlockSpec((1, 1, n_ctx, WIN_HD), lambda bi, kv, i: (bi, kv, 0, 0))
        in_specs += [cspec, cspec]
        args += [kc, vc]
    in_specs += [qspec, _const_spec(sink_b.shape)]
    args += [z, sink_b]
    return pl.pallas_call(
        functools.partial(_sink_attn_kernel, n_ctx=n_ctx, window=window, nb=nb),
        grid=(b, WIN_KV, nb), in_specs=in_specs, out_specs=qspec,
        out_shape=jax.ShapeDtypeStruct((b, t, W_WIN), BF16),
        compiler_params=_cparams("parallel", "parallel", "arbitrary"),
    )(*args)


def _odd_out_kernel(oc_ref, od_ref, x_ref, mod_ref, wo_ref, fg_ref, o_ref):
    out = _dot(oc_ref[0], wo_ref[0:W_DIFF, :]) + _dot(od_ref[0], wo_ref[W_DIFF:W_DIFF + W_WIN, :])
    x = x_ref[0] + mod_ref[0][:, 2 * D:3 * D] * out
    ms = jnp.mean(x * x, axis=-1, keepdims=True)
    o_ref[0] = x * lax.rsqrt(ms + EPS) * fg_ref[...]


def _odd_out(oc, od, x, mod, w_out, final_g, tm):
    b, t, _ = x.shape
    nmod = mod.shape[0]
    mod_map = (lambda bi, i: (bi, 0, 0)) if nmod > 1 else (lambda bi, i: (0, 0, 0))
    tile = pl.BlockSpec((1, tm, D), lambda bi, i: (bi, i, 0))
    return pl.pallas_call(
        _odd_out_kernel,
        grid=(b, t // tm),
        in_specs=[tile, tile, tile, pl.BlockSpec((1, 1, 3 * D), mod_map), _const_spec(w_out.shape),
                  _const_spec((1, D))],
        out_specs=tile,
        out_shape=jax.ShapeDtypeStruct((b, t, D), F32),
        compiler_params=_cparams("parallel", "parallel"),
    )(oc, od, x, mod, w_out, final_g)


def _rope_tables(t_len):
    nf = DIFF_HD // 4
    pos = jnp.arange(t_len)
    row = (pos // GRID_W).astype(F32)
    col = (pos % GRID_W).astype(F32)
    inv = ROPE_THETA ** (-jnp.arange(nf, dtype=F32) / nf)
    ang_r = row[:, None] * inv
    ang_c = col[:, None] * inv
    cos = jnp.concatenate([jnp.cos(ang_r)] * 2 + [jnp.cos(ang_c)] * 2, axis=1)
    sin = jnp.concatenate([-jnp.sin(ang_r), jnp.sin(ang_r), -jnp.sin(ang_c), jnp.sin(ang_c)], axis=1)
    return jnp.tile(cos, (1, 2)), jnp.tile(sin, (1, 2))


def _row_tile(t_len):
    return 512 if t_len % 512 == 0 else 256


def kernel(x_prompt, x_sample, state_ssd_fwd, state_ssd_bwd, cache_diff_k, cache_diff_v, cache_win_k, cache_win_v, c, c_ctx, w_ada, b_ada, ev_w_in, ev_conv_w, ev_conv_b, ev_A_log, ev_dt_bias, ev_D, ev_norm_g, ev_pool_w, ev_pool_scale, ev_w_out, od_w_in, od_lambda, od_subln_g, od_sink, od_w_out, final_norm_g):
    n_lat = x_sample.shape[0]
    rows = -(-(1 + n_lat) // 8) * 8
    cvec = jnp.concatenate([c_ctx[None, :], c, jnp.zeros((rows - 1 - n_lat, D), F32)], axis=0)
    ada = _ada(cvec, w_ada, b_ada)
    mods = [(ada[l, 0:1][:, None, :], ada[l, 1:1 + n_lat][:, None, :]) for l in range(2)]

    w = ev_w_in[0]
    c_dt = D + CONV_CH
    n_dt = 2 * SSD_HEADS
    w_even = jnp.concatenate([w[:, :c_dt], w[:, c_dt + n_dt:], w[:, c_dt:c_dt + n_dt],
                              jnp.zeros((D, 128 - n_dt), F32)], axis=1).astype(BF16)
    pad = jnp.zeros((128 - n_dt,), F32)
    bias_row = jnp.concatenate([ev_dt_bias[0].reshape(-1), pad])
    alog_row = jnp.concatenate([ev_A_log[0].reshape(-1), pad])
    prm = jnp.concatenate([bias_row[None], alog_row[None], jnp.zeros((6, 128), F32)], axis=0)
    prmt = jnp.stack([jnp.broadcast_to(ev_dt_bias[0].reshape(-1, 1), (n_dt, 128)),
                      jnp.broadcast_to(ev_A_log[0].reshape(-1, 1), (n_dt, 128))])
    dsk = jnp.repeat(ev_D[0], SSD_P)[None, :]
    ng = ev_norm_g[0][None, :]
    conv_w = ev_conv_w[0]
    conv_b = ev_conv_b[0][None, :]
    pool_w = ev_pool_w[0].astype(BF16)
    pool_scale = ev_pool_scale[0][None, :]
    w_out_even = ev_w_out[0].astype(BF16)

    def even_layer(x, mod, h0f, h0b, want_final):
        tm = _row_tile(x.shape[1])
        za, xbc, zb, xb, dt = _even_inproj(x, mod, w_even, tm)
        y, hf, hb = _ssd(xbc, dt, za, conv_w, conv_b, prm, prmt, dsk, ng, h0f, h0b, want_final)
        return _even_out(y, xb, zb, x, mod, pool_w, pool_scale, w_out_even, tm), hf, hb

    w_odd = od_w_in[0].astype(BF16)
    w_out_odd = od_w_out[0].astype(BF16)
    lam_init = 0.8 - 0.6 * math.exp(-0.3 * 1)
    lam_p = od_lambda[0]
    subln_g = od_subln_g[0][None, :]
    sink_b = jnp.broadcast_to(od_sink[0][:, None], (WIN_KV * WIN_GROUP, 128))
    final_g = final_norm_g[None, :]

    xc, ssd_f, ssd_b = even_layer(x_prompt, mods[0][0], None, None, True)
    tm = _row_tile(xc.shape[1])
    qc, kc, vc, zc, qd, kd, vd, zd = _odd_inproj(xc, mods[1][0], w_odd, tm, None)
    oc = _diff_attn(qc, kc, vc, None, None, zc, lam_p, subln_g, xc.shape[1], lam_init)
    od = _sink_attn(qd, kd, vd, None, None, zd, sink_b, False)
    y_prompt = _odd_out(oc, od, xc, mods[1][0], w_out_odd, final_g, tm)

    xl, _, _ = even_layer(x_sample, mods[0][1], state_ssd_fwd[:, 0], state_ssd_bwd[:, 0], False)
    t_lat = xl.shape[1]
    tm = _row_tile(t_lat)
    lq, lk, lv, lz, lqd, lkd, lvd, lzd = _odd_inproj(xl, mods[1][1], w_odd, tm, _rope_tables(t_lat))
    loc = _diff_attn(lq, lk, lv, cache_diff_k[:, 0].astype(BF16), cache_diff_v[:, 0].astype(BF16), lz, lam_p,
                     subln_g, 256, lam_init)
    lod = _sink_attn(lqd, lkd, lvd, cache_win_k[:, 0].astype(BF16), cache_win_v[:, 0].astype(BF16), lzd, sink_b,
                     True)
    y_sample = _odd_out(loc, lod, xl, mods[1][1], w_out_odd, final_g, tm)

    return (y_prompt, y_sample, ssd_f[:, None], ssd_b[:, None], kc[:, None], vc[:, None], kd[:, None],
            vd[:, None])
```

```python
import functools
import math

import jax
import jax.numpy as jnp
import numpy as np
from jax import lax
from jax.experimental import pallas as pl
from jax.experimental.pallas import tpu as pltpu

F32 = jnp.float32
BF16 = jnp.bfloat16
HIGHEST = lax.Precision.HIGHEST

D = 1024
EPS = 1e-6
NEG = -1e30
LOG2E = 1.4426950408889634
GRID_W = 64
ROPE_THETA = 10000.0
Q = 128
SSD_HEADS = 16
SSD_P = 64
SSD_N = 128
SSD_GROUPS = 4
D_CONV = 5
CONV_CH = 2048
POOL_WINDOWS = (2, 4, 8, 16)
POOL_GROUP = 256
HALO = 8
DIFF_HEADS = 8
DIFF_HD = 64
WIN_KV = 4
WIN_GROUP = 4
WIN_HD = 64
VMEM_LIMIT = 56 * 1024 * 1024


def _cparams(*sem):
    return pltpu.CompilerParams(dimension_semantics=sem, vmem_limit_bytes=VMEM_LIMIT)


def _dot(a, b, **kw):
    return jnp.dot(a, b, preferred_element_type=F32, **kw)


def _dot_nt(a, b):
    return lax.dot_general(a, b, (((1,), (1,)), ((), ())), preferred_element_type=F32)


def _silu(x):
    return x * jax.nn.sigmoid(x)


def _softplus(x):
    return jnp.maximum(x, 0.0) + jnp.log1p(jnp.exp(-jnp.abs(x)))


def _modnorm(x, mod):
    shift = mod[:, :D]
    scale = mod[:, D:2 * D]
    ms = jnp.mean(x * x, axis=-1, keepdims=True)
    return x * lax.rsqrt(ms + EPS) * (1.0 + scale) + shift


def _const_spec(shape):
    nd = len(shape)
    return pl.BlockSpec(shape, lambda *_: (0,) * nd)


def _ada_kernel(c_ref, w_ref, b_ref, o_ref):
    o_ref[0] = _dot(_silu(c_ref[...]), w_ref[0], precision=HIGHEST) + b_ref[0]


def _ada(cvec, w_ada, b_ada):
    depth, _, n3 = w_ada.shape
    rows = cvec.shape[0]
    tn = 1024
    return pl.pallas_call(
        _ada_kernel,
        grid=(depth, n3 // tn),
        in_specs=[pl.BlockSpec((rows, D), lambda l, j: (0, 0)),
                  pl.BlockSpec((1, D, tn), lambda l, j: (l, 0, j)),
                  pl.BlockSpec((1, 1, tn), lambda l, j: (l, 0, j))],
        out_specs=pl.BlockSpec((1, rows, tn), lambda l, j: (l, 0, j)),
        out_shape=jax.ShapeDtypeStruct((depth, rows, n3), F32),
        compiler_params=_cparams("arbitrary", "arbitrary"),
    )(cvec, w_ada, b_ada.reshape(depth, 1, n3))


EVEN_SPLITS = (1024, 2048, 1024, 1024, 128)


def _even_inproj_kernel(x_ref, mod_ref, w_ref, *out_refs):
    hb = _modnorm(x_ref[0], mod_ref[0]).astype(BF16)
    off = 0
    for o_ref in out_refs:
        wd = o_ref.shape[-1]
        o_ref[0] = _dot(hb, w_ref[:, off:off + wd]).astype(o_ref.dtype)
        off += wd


def _even_inproj(x, mod, w, tm):
    b, t, _ = x.shape
    nmod = mod.shape[0]
    ntot = w.shape[1]
    mod_map = (lambda bi, i: (bi, 0, 0)) if nmod > 1 else (lambda bi, i: (0, 0, 0))
    return pl.pallas_call(
        _even_inproj_kernel,
        grid=(b, t // tm),
        in_specs=[pl.BlockSpec((1, tm, D), lambda bi, i: (bi, i, 0)),
                  pl.BlockSpec((1, 1, 3 * D), mod_map),
                  _const_spec((D, ntot))],
        out_specs=[pl.BlockSpec((1, tm, wd), lambda bi, i: (bi, i, 0)) for wd in EVEN_SPLITS],
        out_shape=[jax.ShapeDtypeStruct((b, t, wd), F32) for wd in EVEN_SPLITS],
        compiler_params=_cparams("parallel", "parallel"),
    )(x, mod, w)


def _fill_halo(ext_ref, main, prev, nxt, has_prev, has_next, rows):
    ext_ref[0:HALO] = jnp.where(has_prev, prev, 0.0)
    ext_ref[HALO:HALO + rows] = main
    ext_ref[HALO + rows:2 * HALO + rows] = jnp.where(has_next, nxt, 0.0)


def _conv_silu(ext_ref, w_ref, b_ref):
    pad = D_CONV // 2
    acc = None
    for k in range(D_CONV):
        term = ext_ref[HALO - pad + k:HALO - pad + k + Q, :] * w_ref[k:k + 1, :]
        acc = term if acc is None else acc + term
    return _silu(acc + b_ref[...])


def _tri_masks():
    row = lax.broadcasted_iota(jnp.int32, (Q, Q), 0)
    col = lax.broadcasted_iota(jnp.int32, (Q, Q), 1)
    return row >= col, col >= row


def _head_expand(vec, lane_base):
    k = lax.broadcasted_iota(jnp.int32, (SSD_N, SSD_HEADS * SSD_P), 0)
    j = lax.broadcasted_iota(jnp.int32, (SSD_N, SSD_HEADS * SSD_P), 1)
    sel = (k == lane_base + j // SSD_P).astype(F32)
    return _dot(jnp.broadcast_to(vec, (8, SSD_N)), sel, precision=HIGHEST)[0:1]


def _pair_cols(arr, base, j, lo):
    return jnp.where(lo, arr[:, base + 2 * j:base + 2 * j + 1], arr[:, base + 2 * j + 1:base + 2 * j + 2])


def _load_state_t(h0_ref, ht_ref):
    for j in range(SSD_HEADS // 2):
        pair = jnp.concatenate([h0_ref[0, 2 * j], h0_ref[0, 2 * j + 1]], axis=0)
        ht_ref[:, j * 128:(j + 1) * 128] = pair.T


def _store_state(ht_ref, out_ref):
    for j in range(SSD_HEADS // 2):
        tr = ht_ref[:, j * 128:(j + 1) * 128].T
        out_ref[0, 2 * j] = tr[0:SSD_P]
        out_ref[0, 2 * j + 1] = tr[SSD_P:2 * SSD_P]


def _ssd_bwd_kernel(*refs, nc, has_h0, want_final):
    it = iter(refs)
    xm_ref, xp_ref, xn_ref, dt_ref, cw_ref, cb_ref, prm_ref = (next(it) for _ in range(7))
    h0_ref = next(it) if has_h0 else None
    hprev_ref = next(it)
    hfin_ref = next(it) if want_final else None
    ht_ref, ext_ref = next(it), next(it)

    i = pl.program_id(1)
    c = nc - 1 - i

    @pl.when(i == 0)
    def _():
        if has_h0:
            _load_state_t(h0_ref, ht_ref)
        else:
            ht_ref[...] = jnp.zeros_like(ht_ref)

    _fill_halo(ext_ref, xm_ref[0], xp_ref[0], xn_ref[0], c > 0, c < nc - 1, Q)
    act = _conv_silu(ext_ref, cw_ref, cb_ref)

    dt = _softplus(dt_ref[0] + prm_ref[0:1, :])
    a = dt * (-jnp.exp(prm_ref[1:2, :]))
    _, upper = _tri_masks()
    acs = _dot(upper.astype(F32), a, precision=HIGHEST)
    wgt = dt * jnp.exp(acs[0:1, :] - acs)
    cd_exp = _head_expand(jnp.exp(acs[0:1, :]), SSD_HEADS)

    lo = lax.broadcasted_iota(jnp.int32, (Q, 128), 1) < SSD_P
    hprev_ref[0, 0] = ht_ref[...]
    for g in range(SSD_GROUPS):
        xw = jnp.concatenate(
            [(act[:, j * 128:(j + 1) * 128] * _pair_cols(wgt, SSD_HEADS, j, lo)).astype(BF16)
             for j in (2 * g, 2 * g + 1)], axis=1)
        bt = act[:, D + g * SSD_N:D + (g + 1) * SSD_N].T.astype(BF16)
        sl = slice(g * 256, (g + 1) * 256)
        ht_ref[:, sl] = ht_ref[:, sl] * cd_exp[:, sl] + _dot(bt, xw)

    if want_final:
        @pl.when(i == nc - 1)
        def _():
            _store_state(ht_ref, hfin_ref)


def _ssd_main_kernel(*refs, nc, has_h0, want_final):
    it = iter(refs)
    (xm_ref, xp_ref, xn_ref, dt_ref, za_ref, hpb_ref, cw_ref, cb_ref, prm_ref, prmt_ref, dsk_ref,
     ng_ref) = (next(it) for _ in range(12))
    h0_ref = next(it) if has_h0 else None
    y_ref = next(it)
    hfin_ref = next(it) if want_final else None
    ht_ref, ext_ref, ybuf_ref = next(it), next(it), next(it)

    c = pl.program_id(1)

    @pl.when(c == 0)
    def _():
        if has_h0:
            _load_state_t(h0_ref, ht_ref)
        else:
            ht_ref[...] = jnp.zeros_like(ht_ref)

    _fill_halo(ext_ref, xm_ref[0], xp_ref[0], xn_ref[0], c > 0, c < nc - 1, Q)
    act = _conv_silu(ext_ref, cw_ref, cb_ref)
    lower, upper = _tri_masks()
    lower_f, upper_f = lower.astype(F32), upper.astype(F32)

    dtr = dt_ref[0]
    dt_c = _softplus(dtr + prm_ref[0:1, :])
    a_c = dt_c * (-jnp.exp(prm_ref[1:2, :]))
    acsf_c = _dot(lower_f, a_c, precision=HIGHEST)
    acsb_c = _dot(upper_f, a_c, precision=HIGHEST)
    ef_c = jnp.exp(acsf_c)
    eb_c = jnp.exp(acsb_c)
    wf_c = dt_c * jnp.exp(acsf_c[Q - 1:Q, :] - acsf_c)
    cdf_exp = _head_expand(jnp.exp(acsf_c[Q - 1:Q, :]), 0)

    nh2 = 2 * SSD_HEADS
    dt_r = _softplus(dtr.T[0:nh2] + prmt_ref[0])
    a_r = dt_r * (-jnp.exp(prmt_ref[1]))
    acsf_r = _dot(a_r, upper_f, precision=HIGHEST)
    acsb_r = _dot(a_r, lower_f, precision=HIGHEST)

    lo = lax.broadcasted_iota(jnp.int32, (Q, 128), 1) < SSD_P
    hpb = hpb_ref[0, 0]
    for g in range(SSD_GROUPS):
        sl = slice(g * 256, (g + 1) * 256)
        cg = act[:, D + 512 + g * SSD_N:D + 512 + (g + 1) * SSD_N].astype(BF16)
        bt = act[:, D + g * SSD_N:D + (g + 1) * SSD_N].T.astype(BF16)
        cbm = _dot(cg, bt)
        hf_old = ht_ref[:, sl]
        yof = _dot(cg, hf_old.astype(BF16))
        yob = _dot(cg, hpb[:, sl].astype(BF16))
        xw = []
        for jj in range(2):
            j = 2 * g + jj
            xs_pair = act[:, j * 128:(j + 1) * 128]
            xs_pair_b = xs_pair.astype(BF16)
            yd = []
            for hh in (2 * j, 2 * j + 1):
                hb_ = SSD_HEADS + hh
                lf = jnp.exp(jnp.where(lower, acsf_c[:, hh:hh + 1] - acsf_r[hh:hh + 1, :], NEG))
                lb = jnp.exp(jnp.where(upper, acsb_c[:, hb_:hb_ + 1] - acsb_r[hb_:hb_ + 1, :], NEG))
                wmat = cbm * (lf * dt_r[hh:hh + 1, :] + lb * dt_r[hb_:hb_ + 1, :])
                yd.append(_dot(wmat.astype(BF16), xs_pair_b))
            y_pair = (jnp.where(lo, yd[0], yd[1])
                      + yof[:, jj * 128:(jj + 1) * 128] * _pair_cols(ef_c, 0, j, lo)
                      + yob[:, jj * 128:(jj + 1) * 128] * _pair_cols(eb_c, SSD_HEADS, j, lo)
                      + xs_pair * dsk_ref[:, j * 128:(j + 1) * 128])
            ybuf_ref[:, j * 128:(j + 1) * 128] = y_pair
            xw.append((xs_pair * _pair_cols(wf_c, 0, j, lo)).astype(BF16))
        ht_ref[:, sl] = hf_old * cdf_exp[:, sl] + _dot(bt, jnp.concatenate(xw, axis=1))

    yz = ybuf_ref[...] * _silu(za_ref[0])
    ms = jnp.mean(yz * yz, axis=-1, keepdims=True)
    y_ref[0] = (yz * lax.rsqrt(ms + EPS) * ng_ref[...]).astype(y_ref.dtype)

    if want_final:
        @pl.when(c == nc - 1)
        def _():
            _store_state(ht_ref, hfin_ref)


def _halo_specs(width, nc, rev):
    rpc = Q // HALO
    nblk = nc * rpc
    cidx = (lambda i: nc - 1 - i) if rev else (lambda i: i)
    return [pl.BlockSpec((1, Q, width), lambda b, i: (b, cidx(i), 0)),
            pl.BlockSpec((1, HALO, width), lambda b, i: (b, jnp.maximum(cidx(i) * rpc - 1, 0), 0)),
            pl.BlockSpec((1, HALO, width), lambda b, i: (b, jnp.minimum((cidx(i) + 1) * rpc, nblk - 1), 0))]


def _ssd(xbc, dt, za, conv_w, conv_b, prm, prmt, dsk, ng, h0f, h0b, want_final):
    b, t, _ = xbc.shape
    nc = t // Q
    has_h0 = h0f is not None
    hw = SSD_HEADS * SSD_P
    state_shape = jax.ShapeDtypeStruct((b, SSD_HEADS, SSD_P, SSD_N), F32)
    state_spec = pl.BlockSpec((1, SSD_HEADS, SSD_P, SSD_N), lambda bi, i: (bi, 0, 0, 0))
    wb = D + SSD_GROUPS * SSD_N

    in_specs = (_halo_specs(wb, nc, True)
                + [pl.BlockSpec((1, Q, 128), lambda bi, i: (bi, nc - 1 - i, 0)),
                   _const_spec((D_CONV, wb)), _const_spec((1, wb)), _const_spec((8, 128))])
    args = [xbc, xbc, xbc, dt, conv_w[:, :wb], conv_b[:, :wb], prm]
    if has_h0:
        in_specs.append(state_spec)
        args.append(h0b)
    out_specs = [pl.BlockSpec((1, 1, SSD_N, hw), lambda bi, i: (bi, nc - 1 - i, 0, 0))]
    out_shape = [jax.ShapeDtypeStruct((b, nc, SSD_N, hw), F32)]
    if want_final:
        out_specs.append(state_spec)
        out_shape.append(state_shape)
    res = pl.pallas_call(
        functools.partial(_ssd_bwd_kernel, nc=nc, has_h0=has_h0, want_final=want_final),
        grid=(b, nc), in_specs=in_specs, out_specs=out_specs, out_shape=out_shape,
        scratch_shapes=[pltpu.VMEM((SSD_N, hw), F32), pltpu.VMEM((Q + 2 * HALO, wb), F32)],
        compiler_params=_cparams("parallel", "arbitrary"),
    )(*args)
    hprev_b = res[0]
    hfin_b = res[1] if want_final else None

    in_specs = (_halo_specs(CONV_CH, nc, False)
                + [pl.BlockSpec((1, Q, 128), lambda bi, i: (bi, i, 0)),
                   pl.BlockSpec((1, Q, hw), lambda bi, i: (bi, i, 0)),
                   pl.BlockSpec((1, 1, SSD_N, hw), lambda bi, i: (bi, i, 0, 0)),
                   _const_spec((D_CONV, CONV_CH)), _const_spec((1, CONV_CH)), _const_spec((8, 128)),
                   _const_spec((2, 2 * SSD_HEADS, 128)), _const_spec((1, hw)), _const_spec((1, hw))])
    args = [xbc, xbc, xbc, dt, za, hprev_b, conv_w, conv_b, prm, prmt, dsk, ng]
    if has_h0:
        in_specs.append(state_spec)
        args.append(h0f)
    out_specs = [pl.BlockSpec((1, Q, hw), lambda bi, i: (bi, i, 0))]
    out_shape = [jax.ShapeDtypeStruct((b, t, hw), BF16)]
    if want_final:
        out_specs.append(state_spec)
        out_shape.append(state_shape)
    res = pl.pallas_call(
        functools.partial(_ssd_main_kernel, nc=nc, has_h0=has_h0, want_final=want_final),
        grid=(b, nc), in_specs=in_specs, out_specs=out_specs, out_shape=out_shape,
        scratch_shapes=[pltpu.VMEM((SSD_N, hw), F32), pltpu.VMEM((Q + 2 * HALO, CONV_CH), F32),
                        pltpu.VMEM((Q, hw), F32)],
        compiler_params=_cparams("parallel", "arbitrary"),
    )(*args)
    return res[0], (res[1] if want_final else None), hfin_b


def _even_out_kernel(y_ref, xm_ref, xp_ref, xn_ref, zb_ref, x_ref, mod_ref, pw_ref, ps_ref, wo_ref, o_ref,
                     ext_ref, *, t_len, tm, nt):
    i = pl.program_id(1)
    _fill_halo(ext_ref, xm_ref[0], xp_ref[0], xn_ref[0], i > 0, i < nt - 1, tm)
    tpos = i * tm + lax.broadcasted_iota(jnp.int32, (tm, POOL_GROUP), 0)
    parts = []
    for g, w in enumerate(POOL_WINDOWS):
        left = w // 2
        right = w - 1 - left
        cols = slice(g * POOL_GROUP, (g + 1) * POOL_GROUP)
        s = None
        for k in range(-left, right + 1):
            v = ext_ref[HALO + k:HALO + k + tm, cols]
            s = v if s is None else s + v
        cnt = (jnp.minimum(tpos + right + 1, t_len) - jnp.maximum(tpos - left, 0)).astype(F32)
        pooled = s / cnt - ext_ref[HALO:HALO + tm, cols]
        parts.append(_dot(pooled.astype(BF16), pw_ref[g]))
    yp = jnp.concatenate(parts, axis=1) * ps_ref[...] * _silu(zb_ref[0])
    out = _dot(y_ref[0], wo_ref[0:D, :]) + _dot(yp.astype(BF16), wo_ref[D:2 * D, :])
    o_ref[0] = x_ref[0] + mod_ref[0][:, 2 * D:3 * D] * out


def _even_out(y, xb, zb, x, mod, pool_w, pool_scale, w_out, tm):
    b, t, _ = x.shape
    nt = t // tm
    rpt = tm // HALO
    nblk = t // HALO
    nmod = mod.shape[0]
    mod_map = (lambda bi, i: (bi, 0, 0)) if nmod > 1 else (lambda bi, i: (0, 0, 0))
    tile = pl.BlockSpec((1, tm, D), lambda bi, i: (bi, i, 0))
    return pl.pallas_call(
        functools.partial(_even_out_kernel, t_len=t, tm=tm, nt=nt),
        grid=(b, nt),
        in_specs=[tile, tile,
                  pl.BlockSpec((1, HALO, D), lambda bi, i: (bi, jnp.maximum(i * rpt - 1, 0), 0)),
                  pl.BlockSpec((1, HALO, D), lambda bi, i: (bi, jnp.minimum((i + 1) * rpt, nblk - 1), 0)),
                  tile, tile, pl.BlockSpec((1, 1, 3 * D), mod_map),
                  _const_spec(pool_w.shape), _const_spec((1, D)), _const_spec((2 * D, D))],
        out_specs=tile,
        out_shape=jax.ShapeDtypeStruct((b, t, D), F32),
        scratch_shapes=[pltpu.VMEM((tm + 2 * HALO, D), F32)],
        compiler_params=_cparams("parallel", "parallel"),
    )(y, xb, xb, xb, zb, x, mod, pool_w, pool_scale, w_out)


W_DIFF = DIFF_HEADS * 2 * DIFF_HD
W_WIN = WIN_KV * WIN_GROUP * WIN_HD
W_KVW = WIN_KV * WIN_HD
Q_SCALE = DIFF_HD ** -0.5 * LOG2E


def _rope(u, cos, sin, first):
    outs = []
    for j in range(u.shape[1] // 128):
        x = u[:, j * 128:(j + 1) * 128]
        partner = jnp.where(first, pltpu.roll(x, 112, 1), pltpu.roll(x, 16, 1))
        outs.append(x * cos + partner * sin)
    return outs[0] if len(outs) == 1 else jnp.concatenate(outs, axis=1)


def _odd_inproj_kernel(*refs, latent):
    it = iter(refs)
    x_ref, mod_ref, w_ref = next(it), next(it), next(it)
    if latent:
        cos_ref, sin_ref = next(it), next(it)
    qc_ref, kc_ref, vc_ref, zc_ref, qd_ref, kd_ref, vd_ref, zd_ref = (next(it) for _ in range(8))

    hb = _modnorm(x_ref[0], mod_ref[0]).astype(BF16)
    if latent:
        cos, sin = cos_ref[...], sin_ref[...]
        first = (lax.broadcasted_iota(jnp.int32, cos.shape, 1) % 32) < 16
        rope = lambda u: _rope(u, cos, sin, first)
    else:
        rope = lambda u: u

    def proj(off, width):
        return _dot(hb, w_ref[:, off:off + width])

    qc_ref[0] = (rope(proj(0, W_DIFF)) * Q_SCALE).astype(qc_ref.dtype)
    kc = rope(proj(W_DIFF, W_DIFF))
    vc = proj(2 * W_DIFF, W_DIFF)
    if latent:
        kc_ref[0] = kc.astype(kc_ref.dtype)
        vc_ref[0] = vc.astype(vc_ref.dtype)
    else:
        for h in range(DIFF_HEADS):
            kc_ref[0, h] = kc[:, h * 128:(h + 1) * 128]
            vc_ref[0, h] = vc[:, h * 128:(h + 1) * 128]
    zc_ref[0] = proj(3 * W_DIFF, W_DIFF)
    off = 4 * W_DIFF
    qd_ref[0] = (rope(proj(off, W_WIN)) * Q_SCALE).astype(qd_ref.dtype)
    kd = rope(proj(off + W_WIN, W_KVW))
    vd = proj(off + W_WIN + W_KVW, W_KVW)
    for kv in range(WIN_KV):
        kd_ref[0, kv] = kd[:, kv * WIN_HD:(kv + 1) * WIN_HD].astype(kd_ref.dtype)
        vd_ref[0, kv] = vd[:, kv * WIN_HD:(kv + 1) * WIN_HD].astype(vd_ref.dtype)
    zd_ref[0] = proj(off + W_WIN + 2 * W_KVW, W_WIN)


def _odd_inproj(x, mod, w, tm, rope_tabs):
    b, t, _ = x.shape
    latent = rope_tabs is not None
    nmod = mod.shape[0]
    mod_map = (lambda bi, i: (bi, 0, 0)) if nmod > 1 else (lambda bi, i: (0, 0, 0))
    tile = lambda wd: pl.BlockSpec((1, tm, wd), lambda bi, i: (bi, i, 0))
    hm = lambda nh, hd: pl.BlockSpec((1, nh, tm, hd), lambda bi, i: (bi, 0, i, 0))
    in_specs = [tile(D), pl.BlockSpec((1, 1, 3 * D), mod_map), _const_spec(w.shape)]
    args = [x, mod, w]
    if latent:
        in_specs += [pl.BlockSpec((tm, 128), lambda bi, i: (i, 0))] * 2
        args += list(rope_tabs)
        kv_specs = [tile(W_DIFF), tile(W_DIFF)]
        kv_shapes = [jax.ShapeDtypeStruct((b, t, W_DIFF), BF16)] * 2
        wdt = BF16
    else:
        kv_specs = [hm(DIFF_HEADS, 128), hm(DIFF_HEADS, 128)]
        kv_shapes = [jax.ShapeDtypeStruct((b, DIFF_HEADS, t, 128), F32)] * 2
        wdt = F32
    out_specs = ([tile(W_DIFF)] + kv_specs + [tile(W_DIFF), tile(W_WIN), hm(WIN_KV, WIN_HD), hm(WIN_KV, WIN_HD),
                                              tile(W_WIN)])
    out_shape = ([jax.ShapeDtypeStruct((b, t, W_DIFF), BF16)] + kv_shapes
                 + [jax.ShapeDtypeStruct((b, t, W_DIFF), F32), jax.ShapeDtypeStruct((b, t, W_WIN), BF16),
                    jax.ShapeDtypeStruct((b, WIN_KV, t, WIN_HD), wdt),
                    jax.ShapeDtypeStruct((b, WIN_KV, t, WIN_HD), wdt),
                    jax.ShapeDtypeStruct((b, t, W_WIN), F32)])
    return pl.pallas_call(
        functools.partial(_odd_inproj_kernel, latent=latent),
        grid=(b, t // tm), in_specs=in_specs, out_specs=out_specs, out_shape=out_shape,
        compiler_params=_cparams("parallel", "parallel"),
    )(*args)


DIFF_SUB = 128


def _diff_attn_kernel(*refs, n_ctx, head_major, lam_init):
    it = iter(refs)
    q_ref, kl_ref, vl_ref = next(it), next(it), next(it)
    if n_ctx:
        kc_ref, vc_ref = next(it), next(it)
    z_ref, lam_ref, g_ref, o_ref, kall_ref, vext_ref = (next(it) for _ in range(6))

    @pl.when(pl.program_id(2) == 0)
    def _():
        kl = kl_ref[0, 0] if head_major else kl_ref[0]
        vl = vl_ref[0, 0] if head_major else vl_ref[0]
        if n_ctx:
            kall_ref[0:n_ctx] = kc_ref[0, 0].astype(BF16)
            vext_ref[0:n_ctx, 0:128] = vc_ref[0, 0].astype(BF16)
        kall_ref[n_ctx:] = kl.astype(BF16)
        vext_ref[n_ctx:, 0:128] = vl.astype(BF16)
        vext_ref[:, 128:256] = jnp.ones((vext_ref.shape[0], 128), BF16)

    lp = lam_ref[...]
    lam = (jnp.exp(jnp.sum(lp[0:1] * lp[1:2], axis=-1, keepdims=True))
           - jnp.exp(jnp.sum(lp[2:3] * lp[3:4], axis=-1, keepdims=True)) + lam_init)
    ts = DIFF_SUB
    lo = lax.broadcasted_iota(jnp.int32, (ts, 128), 1) < DIFF_HD
    zero = jnp.zeros((ts, 128), q_ref.dtype)
    for a in range(q_ref.shape[1] // ts):
        rows = slice(a * ts, (a + 1) * ts)
        q = q_ref[0, rows, :]
        q2 = jnp.concatenate([jnp.where(lo, q, zero), jnp.where(lo, zero, q)], axis=0)
        s = _dot_nt(q2, kall_ref[...])
        m = jnp.max(s, axis=-1, keepdims=True)
        p = jnp.exp2(s - m).astype(BF16)
        oe = _dot(p, vext_ref[...])
        on2 = oe[:, 0:128] / oe[:, 128:129]
        o = on2[0:ts] - lam * on2[ts:2 * ts]
        ms = jnp.mean(o * o, axis=-1, keepdims=True)
        on = o * lax.rsqrt(ms + EPS) * g_ref[...] * (1.0 - lam_init)
        o_ref[0, rows, :] = (on * _silu(z_ref[0, rows, :])).astype(o_ref.dtype)


def _diff_attn(q, k, v, kc, vc, z, lam_p, subln_g, tq, lam_init):
    b, t, _ = q.shape
    head_major = k.ndim == 4
    n_ctx = 0 if kc is None else kc.shape[2]
    nk = n_ctx + t
    qspec = pl.BlockSpec((1, tq, 128), lambda bi, h, i: (bi, i, h))
    if head_major:
        kvspec = pl.BlockSpec((1, 1, t, 128), lambda bi, h, i: (bi, h, 0, 0))
    else:
        kvspec = pl.BlockSpec((1, t, 128), lambda bi, h, i: (bi, 0, h))
    in_specs = [qspec, kvspec, kvspec]
    args = [q, k, v]
    if n_ctx:
        cspec = pl.BlockSpec((1, 1, n_ctx, 128), lambda bi, h, i: (bi, h, 0, 0))
        in_specs += [cspec, cspec]
        args += [kc, vc]
    in_specs += [qspec, _const_spec((4, DIFF_HD)), _const_spec((1, 128))]
    args += [z, lam_p, subln_g]
    return pl.pallas_call(
        functools.partial(_diff_attn_kernel, n_ctx=n_ctx, head_major=head_major, lam_init=lam_init),
        grid=(b, DIFF_HEADS, t // tq), in_specs=in_specs, out_specs=qspec,
        out_shape=jax.ShapeDtypeStruct((b, t, W_DIFF), BF16),
        scratch_shapes=[pltpu.VMEM((nk, 128), BF16), pltpu.VMEM((nk, 256), BF16)],
        compiler_params=_cparams("parallel", "parallel", "arbitrary"),
    )(*args)


WIN_SUB = 2


def _sink_attn_kernel(*refs, n_ctx, window, nb, rows, nq):
    it = iter(refs)
    q_ref = next(it)
    if window:
        kp_ref, kcur_ref, kn_ref, vp_ref, vcur_ref, vn_ref = (next(it) for _ in range(6))
    else:
        kcur_ref, vcur_ref = next(it), next(it)
    if n_ctx:
        kc_ref, vc_ref = next(it), next(it)
    z_ref, sink_ref, o_ref = next(it), next(it), next(it)

    kvh = pl.program_id(1)
    i = pl.program_id(2)
    srows = WIN_GROUP * rows
    if window:
        kwin = jnp.concatenate([kp_ref[0, 0], kcur_ref[0, 0], kn_ref[0, 0]], axis=0).astype(BF16)
        vwin = jnp.concatenate([vp_ref[0, 0], vcur_ref[0, 0], vn_ref[0, 0]], axis=0).astype(BF16)
        r = lax.broadcasted_iota(jnp.int32, (srows, 3 * Q), 0) % rows
        j = lax.broadcasted_iota(jnp.int32, (srows, 3 * Q), 1)
        band = jnp.where((j >= r) & (j <= r + 2 * Q), 0.0, NEG)
        jrow = lax.broadcasted_iota(jnp.int32, (1, 3 * Q), 1)
    else:
        kwin = kcur_ref[0, 0].astype(BF16)
        vwin = vcur_ref[0, 0].astype(BF16)
    if n_ctx:
        kc = kc_ref[0, 0].astype(BF16)
        vc = vc_ref[0, 0].astype(BF16)
    sk = jnp.concatenate(
        [jnp.broadcast_to(sink_ref[pl.ds(kvh * WIN_GROUP + g, 1), :], (rows, 128)) for g in range(WIN_GROUP)],
        axis=0)[:, 0:1] * LOG2E
    for a in range(nq):
        qb = q_ref[0, a * rows:(a + 1) * rows, :]
        q4 = jnp.concatenate([qb[:, g * WIN_HD:(g + 1) * WIN_HD] for g in range(WIN_GROUP)], axis=0)
        if window:
            gb = i * nq + a
            kw = kwin[a * Q:(a + 3) * Q]
            vw = vwin[a * Q:(a + 3) * Q]
            edge = jnp.where(((jrow < Q) & (gb == 0)) | ((jrow >= 2 * Q) & (gb == nb - 1)), NEG, 0.0)
            sw = _dot_nt(q4, kw) + (band + edge)
        else:
            kw, vw = kwin, vwin
            sw = _dot_nt(q4, kw)
        m = jnp.maximum(jnp.max(sw, axis=-1, keepdims=True), sk)
        if n_ctx:
            sc = _dot_nt(q4, kc)
            m = jnp.maximum(m, jnp.max(sc, axis=-1, keepdims=True))
        pw = jnp.exp2(sw - m)
        den = jnp.sum(pw, axis=-1, keepdims=True) + jnp.exp2(sk - m)
        acc = _dot(pw.astype(BF16), vw)
        if n_ctx:
            pc = jnp.exp2(sc - m)
            den = den + jnp.sum(pc, axis=-1, keepdims=True)
            acc = acc + _dot(pc.astype(BF16), vc)
        o4 = acc / den
        ob = jnp.concatenate([o4[g * rows:(g + 1) * rows] for g in range(WIN_GROUP)], axis=1)
        o_ref[0, a * rows:(a + 1) * rows, :] = (ob * _silu(z_ref[0, a * rows:(a + 1) * rows, :])).astype(o_ref.dtype)


def _sink_attn(q, k, v, kc, vc, z, sink_b, window):
    b, t, _ = q.shape
    n_ctx = 0 if kc is None else kc.shape[2]
    gw = WIN_GROUP * WIN_HD
    if window:
        nq, rows, nb = WIN_SUB, Q, t // Q
        nsteps = nb // nq
        blk = lambda n, f: pl.BlockSpec((1, 1, n * Q, WIN_HD), lambda bi, kv, i: (bi, kv, f(i), 0))
        kv_specs = [blk(1, lambda i: jnp.maximum(i * nq - 1, 0)), blk(nq, lambda i: i),
                    blk(1, lambda i: jnp.minimum((i + 1) * nq, nb - 1))]
        in_kv = kv_specs + kv_specs
        args_kv = [k, k, k, v, v, v]
    else:
        nq, rows, nb, nsteps = 1, t, 1, 1
        in_kv = [pl.BlockSpec((1, 1, t, WIN_HD), lambda bi, kv, i: (bi, kv, 0, 0))] * 2
        args_kv = [k, v]
    qspec = pl.BlockSpec((1, nq * rows, gw), lambda bi, kv, i: (bi, i, kv))
    in_specs = [qspec] + in_kv
    args = [q] + args_kv
    if n_ctx:
        cspec = pl.BlockSpec((1, 1, n_ctx, WIN_HD), lambda bi, kv, i: (bi, kv, 0, 0))
        in_specs += [cspec, cspec]
        args += [kc, vc]
    in_specs += [qspec, _const_spec(sink_b.shape)]
    args += [z, sink_b]
    return pl.pallas_call(
        functools.partial(_sink_attn_kernel, n_ctx=n_ctx, window=window, nb=nb, rows=rows, nq=nq),
        grid=(b, WIN_KV, nsteps), in_specs=in_specs, out_specs=qspec,
        out_shape=jax.ShapeDtypeStruct((b, t, W_WIN), BF16),
        compiler_params=_cparams("parallel", "parallel", "arbitrary"),
    )(*args)


def _odd_out_kernel(oc_ref, od_ref, x_ref, mod_ref, wo_ref, fg_ref, o_ref):
    out = _dot(oc_ref[0], wo_ref[0:W_DIFF, :]) + _dot(od_ref[0], wo_ref[W_DIFF:W_DIFF + W_WIN, :])
    x = x_ref[0] + mod_ref[0][:, 2 * D:3 * D] * out
    ms = jnp.mean(x * x, axis=-1, keepdims=True)
    o_ref[0] = x * lax.rsqrt(ms + EPS) * fg_ref[...]


def _odd_out(oc, od, x, mod, w_out, final_g, tm):
    b, t, _ = x.shape
    nmod = mod.shape[0]
    mod_map = (lambda bi, i: (bi, 0, 0)) if nmod > 1 else (lambda bi, i: (0, 0, 0))
    tile = pl.BlockSpec((1, tm, D), lambda bi, i: (bi, i, 0))
    return pl.pallas_call(
        _odd_out_kernel,
        grid=(b, t // tm),
        in_specs=[tile, tile, tile, pl.BlockSpec((1, 1, 3 * D), mod_map), _const_spec(w_out.shape),
                  _const_spec((1, D))],
        out_specs=tile,
        out_shape=jax.ShapeDtypeStruct((b, t, D), F32),
        compiler_params=_cparams("parallel", "parallel"),
    )(oc, od, x, mod, w_out, final_g)


def _rope_tables(t_len):
    nf = DIFF_HD // 4
    pos = jnp.arange(t_len)
    row = (pos // GRID_W).astype(F32)
    col = (pos % GRID_W).astype(F32)
    inv = ROPE_THETA ** (-jnp.arange(nf, dtype=F32) / nf)
    ang_r = row[:, None] * inv
    ang_c = col[:, None] * inv
    cos = jnp.concatenate([jnp.cos(ang_r)] * 2 + [jnp.cos(ang_c)] * 2, axis=1)
    sin = jnp.concatenate([-jnp.sin(ang_r), jnp.sin(ang_r), -jnp.sin(ang_c), jnp.sin(ang_c)], axis=1)
    return jnp.tile(cos, (1, 2)), jnp.tile(sin, (1, 2))


def _row_tile(t_len):
    return 512 if t_len % 512 == 0 else 256


def kernel(x_prompt, x_sample, state_ssd_fwd, state_ssd_bwd, cache_diff_k, cache_diff_v, cache_win_k, cache_win_v, c, c_ctx, w_ada, b_ada, ev_w_in, ev_conv_w, ev_conv_b, ev_A_log, ev_dt_bias, ev_D, ev_norm_g, ev_pool_w, ev_pool_scale, ev_w_out, od_w_in, od_lambda, od_subln_g, od_sink, od_w_out, final_norm_g):
    n_lat = x_sample.shape[0]
    rows = -(-(1 + n_lat) // 8) * 8
    cvec = jnp.concatenate([c_ctx[None, :], c, jnp.zeros((rows - 1 - n_lat, D), F32)], axis=0)
    ada = _ada(cvec, w_ada, b_ada)
    mods = [(ada[l, 0:1][:, None, :], ada[l, 1:1 + n_lat][:, None, :]) for l in range(2)]

    w = ev_w_in[0]
    c_dt = D + CONV_CH
    n_dt = 2 * SSD_HEADS
    w_even = jnp.concatenate([w[:, :c_dt], w[:, c_dt + n_dt:], w[:, c_dt:c_dt + n_dt],
                              jnp.zeros((D, 128 - n_dt), F32)], axis=1).astype(BF16)
    pad = jnp.zeros((128 - n_dt,), F32)
    bias_row = jnp.concatenate([ev_dt_bias[0].reshape(-1), pad])
    alog_row = jnp.concatenate([ev_A_log[0].reshape(-1), pad])
    prm = jnp.concatenate([bias_row[None], alog_row[None], jnp.zeros((6, 128), F32)], axis=0)
    prmt = jnp.stack([jnp.broadcast_to(ev_dt_bias[0].reshape(-1, 1), (n_dt, 128)),
                      jnp.broadcast_to(ev_A_log[0].reshape(-1, 1), (n_dt, 128))])
    dsk = jnp.repeat(ev_D[0], SSD_P)[None, :]
    ng = ev_norm_g[0][None, :]
    conv_w = ev_conv_w[0]
    conv_b = ev_conv_b[0][None, :]
    pool_w = ev_pool_w[0].astype(BF16)
    pool_scale = ev_pool_scale[0][None, :]
    w_out_even = ev_w_out[0].astype(BF16)

    def even_layer(x, mod, h0f, h0b, want_final):
        tm = _row_tile(x.shape[1])
        za, xbc, zb, xb, dt = _even_inproj(x, mod, w_even, tm)
        y, hf, hb = _ssd(xbc, dt, za, conv_w, conv_b, prm, prmt, dsk, ng, h0f, h0b, want_final)
        return _even_out(y, xb, zb, x, mod, pool_w, pool_scale, w_out_even, tm), hf, hb

    w_odd = od_w_in[0].astype(BF16)
    w_out_odd = od_w_out[0].astype(BF16)
    lam_init = 0.8 - 0.6 * math.exp(-0.3 * 1)
    lam_p = od_lambda[0]
    subln_g = od_subln_g[0][None, :]
    sink_b = jnp.broadcast_to(od_sink[0][:, None], (WIN_KV * WIN_GROUP, 128))
    final_g = final_norm_g[None, :]

    xc, ssd_f, ssd_b = even_layer(x_prompt, mods[0][0], None, None, True)
    tm = _row_tile(xc.shape[1])
    qc, kc, vc, zc, qd, kd, vd, zd = _odd_inproj(xc, mods[1][0], w_odd, tm, None)
    oc = _diff_attn(qc, kc, vc, None, None, zc, lam_p, subln_g, xc.shape[1], lam_init)
    od = _sink_attn(qd, kd, vd, None, None, zd, sink_b, False)
    y_prompt = _odd_out(oc, od, xc, mods[1][0], w_out_odd, final_g, tm)

    xl, _, _ = even_layer(x_sample, mods[0][1], state_ssd_fwd[:, 0], state_ssd_bwd[:, 0], False)
    t_lat = xl.shape[1]
    tm = _row_tile(t_lat)
    lq, lk, lv, lz, lqd, lkd, lvd, lzd = _odd_inproj(xl, mods[1][1], w_odd, tm, _rope_tables(t_lat))
    loc = _diff_attn(lq, lk, lv, cache_diff_k[:, 0].astype(BF16), cache_diff_v[:, 0].astype(BF16), lz, lam_p,
                     subln_g, min(1024, t_lat), lam_init)
    lod = _sink_attn(lqd, lkd, lvd, cache_win_k[:, 0].astype(BF16), cache_win_v[:, 0].astype(BF16), lzd, sink_b,
                     True)
    y_sample = _odd_out(loc, lod, xl, mods[1][1], w_out_odd, final_g, tm)

    return (y_prompt, y_sample, ssd_f[:, None], ssd_b[:, None], kc[:, None], vc[:, None], kd[:, None],
            vd[:, None])
```

```python
import functools
import math

import jax
import jax.numpy as jnp
import numpy as np
from jax import lax
from jax.experimental import pallas as pl
from jax.experimental.pallas import tpu as pltpu

F32 = jnp.float32
BF16 = jnp.bfloat16
HIGHEST = lax.Precision.HIGHEST

D = 1024
EPS = 1e-6
NEG = -1e30
LOG2E = 1.4426950408889634
GRID_W = 64
ROPE_THETA = 10000.0
Q = 128
SSD_HEADS = 16
SSD_P = 64
SSD_N = 128
SSD_GROUPS = 4
D_CONV = 5
CONV_CH = 2048
POOL_WINDOWS = (2, 4, 8, 16)
POOL_GROUP = 256
HALO = 8
DIFF_HEADS = 8
DIFF_HD = 64
WIN_KV = 4
WIN_GROUP = 4
WIN_HD = 64
VMEM_LIMIT = 56 * 1024 * 1024


def _cparams(*sem):
    return pltpu.CompilerParams(dimension_semantics=sem, vmem_limit_bytes=VMEM_LIMIT)


def _dot(a, b, **kw):
    return jnp.dot(a, b, preferred_element_type=F32, **kw)


def _dot_nt(a, b):
    return lax.dot_general(a, b, (((1,), (1,)), ((), ())), preferred_element_type=F32)


def _silu(x):
    return x * jax.nn.sigmoid(x)


def _softplus(x):
    return jnp.maximum(x, 0.0) + jnp.log1p(jnp.exp(-jnp.abs(x)))


def _modnorm(x, mod):
    shift = mod[:, :D]
    scale = mod[:, D:2 * D]
    ms = jnp.mean(x * x, axis=-1, keepdims=True)
    return x * lax.rsqrt(ms + EPS) * (1.0 + scale) + shift


def _const_spec(shape):
    nd = len(shape)
    return pl.BlockSpec(shape, lambda *_: (0,) * nd)


def _ada_kernel(c_ref, w_ref, b_ref, o_ref):
    o_ref[0] = _dot(_silu(c_ref[...]), w_ref[0], precision=HIGHEST) + b_ref[0]


def _ada(cvec, w_ada, b_ada):
    depth, _, n3 = w_ada.shape
    rows = cvec.shape[0]
    tn = 1024
    return pl.pallas_call(
        _ada_kernel,
        grid=(depth, n3 // tn),
        in_specs=[pl.BlockSpec((rows, D), lambda l, j: (0, 0)),
                  pl.BlockSpec((1, D, tn), lambda l, j: (l, 0, j)),
                  pl.BlockSpec((1, 1, tn), lambda l, j: (l, 0, j))],
        out_specs=pl.BlockSpec((1, rows, tn), lambda l, j: (l, 0, j)),
        out_shape=jax.ShapeDtypeStruct((depth, rows, n3), F32),
        compiler_params=_cparams("arbitrary", "arbitrary"),
    )(cvec, w_ada, b_ada.reshape(depth, 1, n3))


EVEN_SPLITS = (1024, 2048, 1024, 1024, 128)


def _even_inproj_kernel(x_ref, xp_ref, xn_ref, mod_ref, w_ref, cw_ref, cb_ref,
                        za_ref, act_ref, zb_ref, xb_ref, dt_ref, ext_ref, *, nt):
    i = pl.program_id(1)
    mod = mod_ref[0]
    tm = x_ref.shape[1]
    hb = _modnorm(x_ref[0], mod).astype(BF16)
    off = 0
    for o_ref, wd in zip((za_ref, None, zb_ref, xb_ref, dt_ref), EVEN_SPLITS):
        if o_ref is not None:
            o_ref[0] = _dot(hb, w_ref[:, off:off + wd])
        off += wd
    x_ext = jnp.concatenate([xp_ref[0], x_ref[0], xn_ref[0]], axis=0)
    h_ext = _modnorm(x_ext, mod).astype(BF16)
    u = _dot(h_ext, w_ref[:, EVEN_SPLITS[0]:EVEN_SPLITS[0] + CONV_CH])
    row = lax.broadcasted_iota(jnp.int32, (tm + 2 * HALO, 1), 0)
    keep = ((row >= HALO) | (i > 0)) & ((row < HALO + tm) | (i < nt - 1))
    ext_ref[...] = jnp.where(keep, u, 0.0)
    pad = D_CONV // 2
    acc = None
    for k in range(D_CONV):
        term = ext_ref[HALO - pad + k:HALO - pad + k + tm, :] * cw_ref[k:k + 1, :]
        acc = term if acc is None else acc + term
    act_ref[0] = _silu(acc + cb_ref[...])


def _even_inproj(x, mod, w, conv_w, conv_b, tm):
    b, t, _ = x.shape
    nt = t // tm
    rpt = tm // HALO
    nblk = t // HALO
    nmod = mod.shape[0]
    ntot = w.shape[1]
    mod_map = (lambda bi, i: (bi, 0, 0)) if nmod > 1 else (lambda bi, i: (0, 0, 0))
    return pl.pallas_call(
        functools.partial(_even_inproj_kernel, nt=nt),
        grid=(b, nt),
        in_specs=[pl.BlockSpec((1, tm, D), lambda bi, i: (bi, i, 0)),
                  pl.BlockSpec((1, HALO, D), lambda bi, i: (bi, jnp.maximum(i * rpt - 1, 0), 0)),
                  pl.BlockSpec((1, HALO, D), lambda bi, i: (bi, jnp.minimum((i + 1) * rpt, nblk - 1), 0)),
                  pl.BlockSpec((1, 1, 3 * D), mod_map),
                  _const_spec((D, ntot)), _const_spec((D_CONV, CONV_CH)), _const_spec((1, CONV_CH))],
        out_specs=[pl.BlockSpec((1, tm, wd), lambda bi, i: (bi, i, 0)) for wd in EVEN_SPLITS],
        out_shape=[jax.ShapeDtypeStruct((b, t, wd), F32) for wd in EVEN_SPLITS],
        scratch_shapes=[pltpu.VMEM((tm + 2 * HALO, CONV_CH), F32)],
        compiler_params=_cparams("parallel", "parallel"),
    )(x, x, x, mod, w, conv_w, conv_b)


def _fill_halo(ext_ref, main, prev, nxt, has_prev, has_next, rows):
    ext_ref[0:HALO] = jnp.where(has_prev, prev, 0.0)
    ext_ref[HALO:HALO + rows] = main
    ext_ref[HALO + rows:2 * HALO + rows] = jnp.where(has_next, nxt, 0.0)


def _tri_masks():
    row = lax.broadcasted_iota(jnp.int32, (Q, Q), 0)
    col = lax.broadcasted_iota(jnp.int32, (Q, Q), 1)
    return row >= col, col >= row


def _head_expand(vec, lane_base):
    k = lax.broadcasted_iota(jnp.int32, (SSD_N, SSD_HEADS * SSD_P), 0)
    j = lax.broadcasted_iota(jnp.int32, (SSD_N, SSD_HEADS * SSD_P), 1)
    sel = (k == lane_base + j // SSD_P).astype(F32)
    return _dot(jnp.broadcast_to(vec, (8, SSD_N)), sel, precision=HIGHEST)[0:1]


def _pair_cols(arr, base, j, lo):
    return jnp.where(lo, arr[:, base + 2 * j:base + 2 * j + 1], arr[:, base + 2 * j + 1:base + 2 * j + 2])


def _load_state_t(h0_ref, ht_ref):
    for j in range(SSD_HEADS // 2):
        pair = jnp.concatenate([h0_ref[0, 2 * j], h0_ref[0, 2 * j + 1]], axis=0)
        ht_ref[:, j * 128:(j + 1) * 128] = pair.T


def _store_state(ht_ref, out_ref):
    for j in range(SSD_HEADS // 2):
        tr = ht_ref[:, j * 128:(j + 1) * 128].T
        out_ref[0, 2 * j] = tr[0:SSD_P]
        out_ref[0, 2 * j + 1] = tr[SSD_P:2 * SSD_P]


def _ssd_bwd_kernel(*refs, nc, has_h0, want_final):
    it = iter(refs)
    act_ref, dt_ref, prm_ref = (next(it) for _ in range(3))
    h0_ref = next(it) if has_h0 else None
    hprev_ref = next(it)
    hfin_ref = next(it) if want_final else None
    ht_ref = next(it)

    i = pl.program_id(1)
    c = nc - 1 - i

    @pl.when(i == 0)
    def _():
        if has_h0:
            _load_state_t(h0_ref, ht_ref)
        else:
            ht_ref[...] = jnp.zeros_like(ht_ref)

    act = act_ref[0]

    dt = _softplus(dt_ref[0] + prm_ref[0:1, :])
    a = dt * (-jnp.exp(prm_ref[1:2, :]))
    _, upper = _tri_masks()
    acs = _dot(upper.astype(F32), a, precision=HIGHEST)
    wgt = dt * jnp.exp(acs[0:1, :] - acs)
    cd_exp = _head_expand(jnp.exp(acs[0:1, :]), SSD_HEADS)

    lo = lax.broadcasted_iota(jnp.int32, (Q, 128), 1) < SSD_P
    hprev_ref[0, 0] = ht_ref[...]
    for g in range(SSD_GROUPS):
        xw = jnp.concatenate(
            [(act[:, j * 128:(j + 1) * 128] * _pair_cols(wgt, SSD_HEADS, j, lo)).astype(BF16)
             for j in (2 * g, 2 * g + 1)], axis=1)
        bt = act[:, D + g * SSD_N:D + (g + 1) * SSD_N].T.astype(BF16)
        sl = slice(g * 256, (g + 1) * 256)
        ht_ref[:, sl] = ht_ref[:, sl] * cd_exp[:, sl] + _dot(bt, xw)

    if want_final:
        @pl.when(i == nc - 1)
        def _():
            _store_state(ht_ref, hfin_ref)


def _ssd_main_kernel(*refs, nc, has_h0, want_final):
    it = iter(refs)
    act_ref, dt_ref, za_ref, hpb_ref, prm_ref, prmt_ref, dsk_ref, ng_ref = (next(it) for _ in range(8))
    h0_ref = next(it) if has_h0 else None
    y_ref = next(it)
    hfin_ref = next(it) if want_final else None
    ht_ref, ybuf_ref = next(it), next(it)

    c = pl.program_id(1)

    @pl.when(c == 0)
    def _():
        if has_h0:
            _load_state_t(h0_ref, ht_ref)
        else:
            ht_ref[...] = jnp.zeros_like(ht_ref)

    act = act_ref[0]
    lower, upper = _tri_masks()
    lower_f, upper_f = lower.astype(F32), upper.astype(F32)

    dtr = dt_ref[0]
    dt_c = _softplus(dtr + prm_ref[0:1, :])
    a_c = dt_c * (-jnp.exp(prm_ref[1:2, :]))
    acsf_c = _dot(lower_f, a_c, precision=HIGHEST)
    acsb_c = _dot(upper_f, a_c, precision=HIGHEST)
    ef_c = jnp.exp(acsf_c)
    eb_c = jnp.exp(acsb_c)
    wf_c = dt_c * jnp.exp(acsf_c[Q - 1:Q, :] - acsf_c)
    cdf_exp = _head_expand(jnp.exp(acsf_c[Q - 1:Q, :]), 0)

    nh2 = 2 * SSD_HEADS
    dt_r = _softplus(dtr.T[0:nh2] + prmt_ref[0])
    a_r = dt_r * (-jnp.exp(prmt_ref[1]))
    acsf_r = _dot(a_r, upper_f, precision=HIGHEST)
    acsb_r = _dot(a_r, lower_f, precision=HIGHEST)

    lgf_c, lgb_c = acsf_c * LOG2E, acsb_c * LOG2E
    lg_dt = jnp.log2(dt_r)
    lgf_r, lgb_r = acsf_r * LOG2E - lg_dt, acsb_r * LOG2E - lg_dt

    lo = lax.broadcasted_iota(jnp.int32, (Q, 128), 1) < SSD_P
    hpb = hpb_ref[0, 0]
    for g in range(SSD_GROUPS):
        sl = slice(g * 256, (g + 1) * 256)
        cg = act[:, D + 512 + g * SSD_N:D + 512 + (g + 1) * SSD_N].astype(BF16)
        bt = act[:, D + g * SSD_N:D + (g + 1) * SSD_N].T.astype(BF16)
        cbm = _dot(cg, bt)
        hf_old = ht_ref[:, sl]
        yof = _dot(cg, hf_old.astype(BF16))
        yob = _dot(cg, hpb[:, sl].astype(BF16))
        xw = []
        for jj in range(2):
            j = 2 * g + jj
            xs_pair = act[:, j * 128:(j + 1) * 128]
            xs_pair_b = xs_pair.astype(BF16)
            yd = []
            for hh in (2 * j, 2 * j + 1):
                hb_ = SSD_HEADS + hh
                lf = jnp.exp2(jnp.where(lower, lgf_c[:, hh:hh + 1] - lgf_r[hh:hh + 1, :], NEG))
                lb = jnp.exp2(jnp.where(upper, lgb_c[:, hb_:hb_ + 1] - lgb_r[hb_:hb_ + 1, :], NEG))
                wmat = cbm * (lf + lb)
                yd.append(_dot(wmat.astype(BF16), xs_pair_b))
            y_pair = (jnp.where(lo, yd[0], yd[1])
                      + yof[:, jj * 128:(jj + 1) * 128] * _pair_cols(ef_c, 0, j, lo)
                      + yob[:, jj * 128:(jj + 1) * 128] * _pair_cols(eb_c, SSD_HEADS, j, lo)
                      + xs_pair * dsk_ref[:, j * 128:(j + 1) * 128])
            ybuf_ref[:, j * 128:(j + 1) * 128] = y_pair
            xw.append((xs_pair * _pair_cols(wf_c, 0, j, lo)).astype(BF16))
        ht_ref[:, sl] = hf_old * cdf_exp[:, sl] + _dot(bt, jnp.concatenate(xw, axis=1))

    yz = ybuf_ref[...] * _silu(za_ref[0])
    ms = jnp.mean(yz * yz, axis=-1, keepdims=True)
    y_ref[0] = (yz * lax.rsqrt(ms + EPS) * ng_ref[...]).astype(y_ref.dtype)

    if want_final:
        @pl.when(c == nc - 1)
        def _():
            _store_state(ht_ref, hfin_ref)


def _ssd(act, dt, za, prm, prmt, dsk, ng, h0f, h0b, want_final):
    b, t, _ = act.shape
    nc = t // Q
    has_h0 = h0f is not None
    hw = SSD_HEADS * SSD_P
    state_shape = jax.ShapeDtypeStruct((b, SSD_HEADS, SSD_P, SSD_N), F32)
    state_spec = pl.BlockSpec((1, SSD_HEADS, SSD_P, SSD_N), lambda bi, i: (bi, 0, 0, 0))
    wb = D + SSD_GROUPS * SSD_N

    in_specs = [pl.BlockSpec((1, Q, wb), lambda bi, i: (bi, nc - 1 - i, 0)),
                pl.BlockSpec((1, Q, 128), lambda bi, i: (bi, nc - 1 - i, 0)),
                _const_spec((8, 128))]
    args = [act, dt, prm]
    if has_h0:
        in_specs.append(state_spec)
        args.append(h0b)
    out_specs = [pl.BlockSpec((1, 1, SSD_N, hw), lambda bi, i: (bi, nc - 1 - i, 0, 0))]
    out_shape = [jax.ShapeDtypeStruct((b, nc, SSD_N, hw), F32)]
    if want_final:
        out_specs.append(state_spec)
        out_shape.append(state_shape)
    res = pl.pallas_call(
        functools.partial(_ssd_bwd_kernel, nc=nc, has_h0=has_h0, want_final=want_final),
        grid=(b, nc), in_specs=in_specs, out_specs=out_specs, out_shape=out_shape,
        scratch_shapes=[pltpu.VMEM((SSD_N, hw), F32)],
        compiler_params=_cparams("parallel", "arbitrary"),
    )(*args)
    hprev_b = res[0]
    hfin_b = res[1] if want_final else None

    in_specs = [pl.BlockSpec((1, Q, CONV_CH), lambda bi, i: (bi, i, 0)),
                pl.BlockSpec((1, Q, 128), lambda bi, i: (bi, i, 0)),
                pl.BlockSpec((1, Q, hw), lambda bi, i: (bi, i, 0)),
                pl.BlockSpec((1, 1, SSD_N, hw), lambda bi, i: (bi, i, 0, 0)),
                _const_spec((8, 128)), _const_spec((2, 2 * SSD_HEADS, 128)), _const_spec((1, hw)),
                _const_spec((1, hw))]
    args = [act, dt, za, hprev_b, prm, prmt, dsk, ng]
    if has_h0:
        in_specs.append(state_spec)
        args.append(h0f)
    out_specs = [pl.BlockSpec((1, Q, hw), lambda bi, i: (bi, i, 0))]
    out_shape = [jax.ShapeDtypeStruct((b, t, hw), BF16)]
    if want_final:
        out_specs.append(state_spec)
        out_shape.append(state_shape)
    res = pl.pallas_call(
        functools.partial(_ssd_main_kernel, nc=nc, has_h0=has_h0, want_final=want_final),
        grid=(b, nc), in_specs=in_specs, out_specs=out_specs, out_shape=out_shape,
        scratch_shapes=[pltpu.VMEM((SSD_N, hw), F32), pltpu.VMEM((Q, hw), F32)],
        compiler_params=_cparams("parallel", "arbitrary"),
    )(*args)
    return res[0], (res[1] if want_final else None), hfin_b


def _even_out_kernel(y_ref, xm_ref, xp_ref, xn_ref, zb_ref, x_ref, mod_ref, pw_ref, ps_ref, wo_ref, o_ref,
                     ext_ref, *, t_len, tm, nt):
    i = pl.program_id(1)
    _fill_halo(ext_ref, xm_ref[0], xp_ref[0], xn_ref[0], i > 0, i < nt - 1, tm)
    tpos = i * tm + lax.broadcasted_iota(jnp.int32, (tm, POOL_GROUP), 0)
    parts = []
    for g, w in enumerate(POOL_WINDOWS):
        left = w // 2
        right = w - 1 - left
        cols = slice(g * POOL_GROUP, (g + 1) * POOL_GROUP)
        s = None
        for k in range(-left, right + 1):
            v = ext_ref[HALO + k:HALO + k + tm, cols]
            s = v if s is None else s + v
        cnt = (jnp.minimum(tpos + right + 1, t_len) - jnp.maximum(tpos - left, 0)).astype(F32)
        pooled = s / cnt - ext_ref[HALO:HALO + tm, cols]
        parts.append(_dot(pooled.astype(BF16), pw_ref[g]))
    yp = jnp.concatenate(parts, axis=1) * ps_ref[...] * _silu(zb_ref[0])
    out = _dot(y_ref[0], wo_ref[0:D, :]) + _dot(yp.astype(BF16), wo_ref[D:2 * D, :])
    o_ref[0] = x_ref[0] + mod_ref[0][:, 2 * D:3 * D] * out


def _even_out(y, xb, zb, x, mod, pool_w, pool_scale, w_out, tm):
    b, t, _ = x.shape
    nt = t // tm
    rpt = tm // HALO
    nblk = t // HALO
    nmod = mod.shape[0]
    mod_map = (lambda bi, i: (bi, 0, 0)) if nmod > 1 else (lambda bi, i: (0, 0, 0))
    tile = pl.BlockSpec((1, tm, D), lambda bi, i: (bi, i, 0))
    return pl.pallas_call(
        functools.partial(_even_out_kernel, t_len=t, tm=tm, nt=nt),
        grid=(b, nt),
        in_specs=[tile, tile,
                  pl.BlockSpec((1, HALO, D), lambda bi, i: (bi, jnp.maximum(i * rpt - 1, 0), 0)),
                  pl.BlockSpec((1, HALO, D), lambda bi, i: (bi, jnp.minimum((i + 1) * rpt, nblk - 1), 0)),
                  tile, tile, pl.BlockSpec((1, 1, 3 * D), mod_map),
                  _const_spec(pool_w.shape), _const_spec((1, D)), _const_spec((2 * D, D))],
        out_specs=tile,
        out_shape=jax.ShapeDtypeStruct((b, t, D), F32),
        scratch_shapes=[pltpu.VMEM((tm + 2 * HALO, D), F32)],
        compiler_params=_cparams("parallel", "parallel"),
    )(y, xb, xb, xb, zb, x, mod, pool_w, pool_scale, w_out)


W_DIFF = DIFF_HEADS * 2 * DIFF_HD
W_WIN = WIN_KV * WIN_GROUP * WIN_HD
W_KVW = WIN_KV * WIN_HD
Q_SCALE = DIFF_HD ** -0.5 * LOG2E


def _rope(u, cos, sin, first):
    outs = []
    for j in range(u.shape[1] // 128):
        x = u[:, j * 128:(j + 1) * 128]
        partner = jnp.where(first, pltpu.roll(x, 112, 1), pltpu.roll(x, 16, 1))
        outs.append(x * cos + partner * sin)
    return outs[0] if len(outs) == 1 else jnp.concatenate(outs, axis=1)


def _odd_inproj_kernel(*refs, latent):
    it = iter(refs)
    x_ref, mod_ref, w_ref = next(it), next(it), next(it)
    if latent:
        cos_ref, sin_ref = next(it), next(it)
    qc_ref, kc_ref, vc_ref, zc_ref, qd_ref, kd_ref, vd_ref, zd_ref = (next(it) for _ in range(8))

    hb = _modnorm(x_ref[0], mod_ref[0]).astype(BF16)
    if latent:
        cos, sin = cos_ref[...], sin_ref[...]
        first = (lax.broadcasted_iota(jnp.int32, cos.shape, 1) % 32) < 16
        rope = lambda u: _rope(u, cos, sin, first)
    else:
        rope = lambda u: u

    def proj(off, width):
        return _dot(hb, w_ref[:, off:off + width])

    qc_ref[0] = (rope(proj(0, W_DIFF)) * Q_SCALE).astype(qc_ref.dtype)
    kc = rope(proj(W_DIFF, W_DIFF))
    vc = proj(2 * W_DIFF, W_DIFF)
    if latent:
        kc_ref[0] = kc.astype(kc_ref.dtype)
        vc_ref[0] = vc.astype(vc_ref.dtype)
    else:
        for h in range(DIFF_HEADS):
            kc_ref[0, h] = kc[:, h * 128:(h + 1) * 128]
            vc_ref[0, h] = vc[:, h * 128:(h + 1) * 128]
    zc_ref[0] = proj(3 * W_DIFF, W_DIFF)
    off = 4 * W_DIFF
    qd_ref[0] = (rope(proj(off, W_WIN)) * Q_SCALE).astype(qd_ref.dtype)
    kd = rope(proj(off + W_WIN, W_KVW))
    vd = proj(off + W_WIN + W_KVW, W_KVW)
    for kv in range(WIN_KV):
        kd_ref[0, kv] = kd[:, kv * WIN_HD:(kv + 1) * WIN_HD].astype(kd_ref.dtype)
        vd_ref[0, kv] = vd[:, kv * WIN_HD:(kv + 1) * WIN_HD].astype(vd_ref.dtype)
    zd_ref[0] = proj(off + W_WIN + 2 * W_KVW, W_WIN)


def _odd_inproj(x, mod, w, tm, rope_tabs):
    b, t, _ = x.shape
    latent = rope_tabs is not None
    nmod = mod.shape[0]
    mod_map = (lambda bi, i: (bi, 0, 0)) if nmod > 1 else (lambda bi, i: (0, 0, 0))
    tile = lambda wd: pl.BlockSpec((1, tm, wd), lambda bi, i: (bi, i, 0))
    hm = lambda nh, hd: pl.BlockSpec((1, nh, tm, hd), lambda bi, i: (bi, 0, i, 0))
    in_specs = [tile(D), pl.BlockSpec((1, 1, 3 * D), mod_map), _const_spec(w.shape)]
    args = [x, mod, w]
    if latent:
        in_specs += [pl.BlockSpec((tm, 128), lambda bi, i: (i, 0))] * 2
        args += list(rope_tabs)
        kv_specs = [tile(W_DIFF), tile(W_DIFF)]
        kv_shapes = [jax.ShapeDtypeStruct((b, t, W_DIFF), BF16)] * 2
        wdt = BF16
    else:
        kv_specs = [hm(DIFF_HEADS, 128), hm(DIFF_HEADS, 128)]
        kv_shapes = [jax.ShapeDtypeStruct((b, DIFF_HEADS, t, 128), F32)] * 2
        wdt = F32
    out_specs = ([tile(W_DIFF)] + kv_specs + [tile(W_DIFF), tile(W_WIN), hm(WIN_KV, WIN_HD), hm(WIN_KV, WIN_HD),
                                              tile(W_WIN)])
    out_shape = ([jax.ShapeDtypeStruct((b, t, W_DIFF), BF16)] + kv_shapes
                 + [jax.ShapeDtypeStruct((b, t, W_DIFF), F32), jax.ShapeDtypeStruct((b, t, W_WIN), BF16),
                    jax.ShapeDtypeStruct((b, WIN_KV, t, WIN_HD), wdt),
                    jax.ShapeDtypeStruct((b, WIN_KV, t, WIN_HD), wdt),
                    jax.ShapeDtypeStruct((b, t, W_WIN), F32)])
    return pl.pallas_call(
        functools.partial(_odd_inproj_kernel, latent=latent),
        grid=(b, t // tm), in_specs=in_specs, out_specs=out_specs, out_shape=out_shape,
        compiler_params=_cparams("parallel", "parallel"),
    )(*args)


DIFF_SUB = 128


def _diff_attn_kernel(*refs, n_ctx, head_major, lam_init, hps):
    it = iter(refs)
    q_ref, kl_ref, vl_ref = next(it), next(it), next(it)
    if n_ctx:
        kc_ref, vc_ref = next(it), next(it)
    z_ref, lam_ref, g_ref, o_ref, kall_ref, vext_ref = (next(it) for _ in range(6))

    @pl.when(pl.program_id(2) == 0)
    def _():
        for hh in range(hps):
            kl = kl_ref[0, hh] if head_major else kl_ref[0]
            vl = vl_ref[0, hh] if head_major else vl_ref[0]
            if n_ctx:
                kall_ref[hh, 0:n_ctx] = kc_ref[0, hh].astype(BF16)
                vext_ref[hh, 0:n_ctx, 0:128] = vc_ref[0, hh].astype(BF16)
            kall_ref[hh, n_ctx:] = kl.astype(BF16)
            vext_ref[hh, n_ctx:, 0:128] = vl.astype(BF16)
            vext_ref[hh, :, 128:256] = jnp.ones((vext_ref.shape[1], 128), BF16)

    lp = lam_ref[...]
    lam = (jnp.exp(jnp.sum(lp[0:1] * lp[1:2], axis=-1, keepdims=True))
           - jnp.exp(jnp.sum(lp[2:3] * lp[3:4], axis=-1, keepdims=True)) + lam_init)
    ts = DIFF_SUB
    lo = lax.broadcasted_iota(jnp.int32, (ts, 128), 1) < DIFF_HD
    zero = jnp.zeros((ts, 128), q_ref.dtype)
    for hh in range(hps):
        cols = slice(hh * 128, (hh + 1) * 128)
        for a in range(q_ref.shape[1] // ts):
            rows = slice(a * ts, (a + 1) * ts)
            q = q_ref[0, rows, cols]
            q2 = jnp.concatenate([jnp.where(lo, q, zero), jnp.where(lo, zero, q)], axis=0)
            s = _dot_nt(q2, kall_ref[hh])
            m = jnp.max(s, axis=-1, keepdims=True)
            p = jnp.exp2(s - m).astype(BF16)
            oe = _dot(p, vext_ref[hh])
            on2 = oe[:, 0:128] / oe[:, 128:129]
            o = on2[0:ts] - lam * on2[ts:2 * ts]
            ms = jnp.mean(o * o, axis=-1, keepdims=True)
            on = o * lax.rsqrt(ms + EPS) * g_ref[...] * (1.0 - lam_init)
            o_ref[0, rows, cols] = (on * _silu(z_ref[0, rows, cols])).astype(o_ref.dtype)


def _diff_attn(q, k, v, kc, vc, z, lam_p, subln_g, tq, lam_init, hps=1):
    b, t, _ = q.shape
    head_major = k.ndim == 4
    assert head_major or hps == 1
    n_ctx = 0 if kc is None else kc.shape[2]
    nk = n_ctx + t
    qspec = pl.BlockSpec((1, tq, hps * 128), lambda bi, h, i: (bi, i, h))
    if head_major:
        kvspec = pl.BlockSpec((1, hps, t, 128), lambda bi, h, i: (bi, h, 0, 0))
    else:
        kvspec = pl.BlockSpec((1, t, 128), lambda bi, h, i: (bi, 0, h))
    in_specs = [qspec, kvspec, kvspec]
    args = [q, k, v]
    if n_ctx:
        cspec = pl.BlockSpec((1, hps, n_ctx, 128), lambda bi, h, i: (bi, h, 0, 0))
        in_specs += [cspec, cspec]
        args += [kc, vc]
    in_specs += [qspec, _const_spec((4, DIFF_HD)), _const_spec((1, 128))]
    args += [z, lam_p, subln_g]
    return pl.pallas_call(
        functools.partial(_diff_attn_kernel, n_ctx=n_ctx, head_major=head_major, lam_init=lam_init, hps=hps),
        grid=(b, DIFF_HEADS // hps, t // tq), in_specs=in_specs, out_specs=qspec,
        out_shape=jax.ShapeDtypeStruct((b, t, W_DIFF), BF16),
        scratch_shapes=[pltpu.VMEM((hps, nk, 128), BF16), pltpu.VMEM((hps, nk, 256), BF16)],
        compiler_params=_cparams("parallel", "parallel", "arbitrary"),
    )(*args)


WIN_SUB = 2


def _sink_attn_kernel(*refs, n_ctx, window, nb, rows, nq, kps):
    it = iter(refs)
    q_ref = next(it)
    if window:
        kp_ref, kcur_ref, kn_ref, vp_ref, vcur_ref, vn_ref = (next(it) for _ in range(6))
    else:
        kcur_ref, vcur_ref = next(it), next(it)
    if n_ctx:
        kc_ref, vc_ref = next(it), next(it)
    z_ref, sink_ref, o_ref = next(it), next(it), next(it)

    i = pl.program_id(2)
    srows = WIN_GROUP * rows
    gw = WIN_GROUP * WIN_HD
    if window:
        r = lax.broadcasted_iota(jnp.int32, (srows, 3 * Q), 0) % rows
        j = lax.broadcasted_iota(jnp.int32, (srows, 3 * Q), 1)
        band = jnp.where((j >= r) & (j <= r + 2 * Q), 0.0, NEG)
        jrow = lax.broadcasted_iota(jnp.int32, (1, 3 * Q), 1)
    for kk in range(kps):
        kvh = pl.program_id(1) * kps + kk
        if window:
            kwin = jnp.concatenate([kp_ref[0, kk], kcur_ref[0, kk], kn_ref[0, kk]], axis=0).astype(BF16)
            vwin = jnp.concatenate([vp_ref[0, kk], vcur_ref[0, kk], vn_ref[0, kk]], axis=0).astype(BF16)
        else:
            kwin = kcur_ref[0, kk].astype(BF16)
            vwin = vcur_ref[0, kk].astype(BF16)
        if n_ctx:
            kc = kc_ref[0, kk].astype(BF16)
            vc = vc_ref[0, kk].astype(BF16)
        sk = jnp.concatenate(
            [jnp.broadcast_to(sink_ref[pl.ds(kvh * WIN_GROUP + g, 1), :], (rows, 128)) for g in range(WIN_GROUP)],
            axis=0)[:, 0:1] * LOG2E
        for a in range(nq):
            rs = slice(a * rows, (a + 1) * rows)
            qb = q_ref[0, rs, kk * gw:(kk + 1) * gw]
            q4 = jnp.concatenate([qb[:, g * WIN_HD:(g + 1) * WIN_HD] for g in range(WIN_GROUP)], axis=0)
            if window:
                gb = i * nq + a
                kw = kwin[a * Q:(a + 3) * Q]
                vw = vwin[a * Q:(a + 3) * Q]
                edge = jnp.where(((jrow < Q) & (gb == 0)) | ((jrow >= 2 * Q) & (gb == nb - 1)), NEG, 0.0)
                sw = _dot_nt(q4, kw) + (band + edge)
            else:
                kw, vw = kwin, vwin
                sw = _dot_nt(q4, kw)
            m = jnp.maximum(jnp.max(sw, axis=-1, keepdims=True), sk)
            if n_ctx:
                sc = _dot_nt(q4, kc)
                m = jnp.maximum(m, jnp.max(sc, axis=-1, keepdims=True))
            pw = jnp.exp2(sw - m)
            den = jnp.sum(pw, axis=-1, keepdims=True) + jnp.exp2(sk - m)
            acc = _dot(pw.astype(BF16), vw)
            if n_ctx:
                pc = jnp.exp2(sc - m)
                den = den + jnp.sum(pc, axis=-1, keepdims=True)
                acc = acc + _dot(pc.astype(BF16), vc)
            o4 = acc / den
            ob = jnp.concatenate([o4[g * rows:(g + 1) * rows] for g in range(WIN_GROUP)], axis=1)
            o_ref[0, rs, kk * gw:(kk + 1) * gw] = (ob * _silu(z_ref[0, rs, kk * gw:(kk + 1) * gw])).astype(o_ref.dtype)


def _sink_attn(q, k, v, kc, vc, z, sink_b, window):
    b, t, _ = q.shape
    n_ctx = 0 if kc is None else kc.shape[2]
    gw = WIN_GROUP * WIN_HD
    if window:
        nq, rows, nb = WIN_SUB, Q, t // Q
        nsteps = nb // nq
        blk = lambda n, f: pl.BlockSpec((1, 1, n * Q, WIN_HD), lambda bi, kv, i: (bi, kv, f(i), 0))
        kv_specs = [blk(1, lambda i: jnp.maximum(i * nq - 1, 0)), blk(nq, lambda i: i),
                    blk(1, lambda i: jnp.minimum((i + 1) * nq, nb - 1))]
        in_kv = kv_specs + kv_specs
        args_kv = [k, k, k, v, v, v]
        kps = 1
    else:
        nq, rows, nb, nsteps, kps = 1, t, 1, 1, WIN_KV
        in_kv = [pl.BlockSpec((1, kps, t, WIN_HD), lambda bi, kv, i: (bi, kv, 0, 0))] * 2
        args_kv = [k, v]
    qspec = pl.BlockSpec((1, nq * rows, kps * gw), lambda bi, kv, i: (bi, i, kv))
    in_specs = [qspec] + in_kv
    args = [q] + args_kv
    if n_ctx:
        cspec = pl.BlockSpec((1, 1, n_ctx, WIN_HD), lambda bi, kv, i: (bi, kv, 0, 0))
        in_specs += [cspec, cspec]
        args += [kc, vc]
    in_specs += [qspec, _const_spec(sink_b.shape)]
    args += [z, sink_b]
    return pl.pallas_call(
        functools.partial(_sink_attn_kernel, n_ctx=n_ctx, window=window, nb=nb, rows=rows, nq=nq, kps=kps),
        grid=(b, WIN_KV // kps, nsteps), in_specs=in_specs, out_specs=qspec,
        out_shape=jax.ShapeDtypeStruct((b, t, W_WIN), BF16),
        compiler_params=_cparams("parallel", "parallel", "arbitrary"),
    )(*args)


def _odd_out_kernel(oc_ref, od_ref, x_ref, mod_ref, wo_ref, fg_ref, o_ref):
    out = _dot(oc_ref[0], wo_ref[0:W_DIFF, :]) + _dot(od_ref[0], wo_ref[W_DIFF:W_DIFF + W_WIN, :])
    x = x_ref[0] + mod_ref[0][:, 2 * D:3 * D] * out
    ms = jnp.mean(x * x, axis=-1, keepdims=True)
    o_ref[0] = x * lax.rsqrt(ms + EPS) * fg_ref[...]


def _odd_out(oc, od, x, mod, w_out, final_g, tm):
    b, t, _ = x.shape
    nmod = mod.shape[0]
    mod_map = (lambda bi, i: (bi, 0, 0)) if nmod > 1 else (lambda bi, i: (0, 0, 0))
    tile = pl.BlockSpec((1, tm, D), lambda bi, i: (bi, i, 0))
    return pl.pallas_call(
        _odd_out_kernel,
        grid=(b, t // tm),
        in_specs=[tile, tile, tile, pl.BlockSpec((1, 1, 3 * D), mod_map), _const_spec(w_out.shape),
                  _const_spec((1, D))],
        out_specs=tile,
        out_shape=jax.ShapeDtypeStruct((b, t, D), F32),
        compiler_params=_cparams("parallel", "parallel"),
    )(oc, od, x, mod, w_out, final_g)


def _rope_tables(t_len):
    nf = DIFF_HD // 4
    pos = jnp.arange(t_len)
    row = (pos // GRID_W).astype(F32)
    col = (pos % GRID_W).astype(F32)
    inv = ROPE_THETA ** (-jnp.arange(nf, dtype=F32) / nf)
    ang_r = row[:, None] * inv
    ang_c = col[:, None] * inv
    cos = jnp.concatenate([jnp.cos(ang_r)] * 2 + [jnp.cos(ang_c)] * 2, axis=1)
    sin = jnp.concatenate([-jnp.sin(ang_r), jnp.sin(ang_r), -jnp.sin(ang_c), jnp.sin(ang_c)], axis=1)
    return jnp.tile(cos, (1, 2)), jnp.tile(sin, (1, 2))


def _row_tile(t_len):
    return 512 if t_len % 512 == 0 else 256


def kernel(x_prompt, x_sample, state_ssd_fwd, state_ssd_bwd, cache_diff_k, cache_diff_v, cache_win_k, cache_win_v, c, c_ctx, w_ada, b_ada, ev_w_in, ev_conv_w, ev_conv_b, ev_A_log, ev_dt_bias, ev_D, ev_norm_g, ev_pool_w, ev_pool_scale, ev_w_out, od_w_in, od_lambda, od_subln_g, od_sink, od_w_out, final_norm_g):
    n_lat = x_sample.shape[0]
    rows = -(-(1 + n_lat) // 8) * 8
    cvec = jnp.concatenate([c_ctx[None, :], c, jnp.zeros((rows - 1 - n_lat, D), F32)], axis=0)
    ada = _ada(cvec, w_ada, b_ada)
    mods = [(ada[l, 0:1][:, None, :], ada[l, 1:1 + n_lat][:, None, :]) for l in range(2)]

    w = ev_w_in[0]
    c_dt = D + CONV_CH
    n_dt = 2 * SSD_HEADS
    w_even = jnp.concatenate([w[:, :c_dt], w[:, c_dt + n_dt:], w[:, c_dt:c_dt + n_dt],
                              jnp.zeros((D, 128 - n_dt), F32)], axis=1).astype(BF16)
    pad = jnp.zeros((128 - n_dt,), F32)
    bias_row = jnp.concatenate([ev_dt_bias[0].reshape(-1), pad])
    alog_row = jnp.concatenate([ev_A_log[0].reshape(-1), pad])
    prm = jnp.concatenate([bias_row[None], alog_row[None], jnp.zeros((6, 128), F32)], axis=0)
    prmt = jnp.stack([jnp.broadcast_to(ev_dt_bias[0].reshape(-1, 1), (n_dt, 128)),
                      jnp.broadcast_to(ev_A_log[0].reshape(-1, 1), (n_dt, 128))])
    dsk = jnp.repeat(ev_D[0], SSD_P)[None, :]
    ng = ev_norm_g[0][None, :]
    conv_w = ev_conv_w[0]
    conv_b = ev_conv_b[0][None, :]
    pool_w = ev_pool_w[0].astype(BF16)
    pool_scale = ev_pool_scale[0][None, :]
    w_out_even = ev_w_out[0].astype(BF16)

    def even_layer(x, mod, h0f, h0b, want_final):
        tm = _row_tile(x.shape[1])
        za, act, zb, xb, dt = _even_inproj(x, mod, w_even, conv_w, conv_b, tm)
        y, hf, hb = _ssd(act, dt, za, prm, prmt, dsk, ng, h0f, h0b, want_final)
        return _even_out(y, xb, zb, x, mod, pool_w, pool_scale, w_out_even, tm), hf, hb

    w_odd = od_w_in[0].astype(BF16)
    w_out_odd = od_w_out[0].astype(BF16)
    lam_init = 0.8 - 0.6 * math.exp(-0.3 * 1)
    lam_p = od_lambda[0]
    subln_g = od_subln_g[0][None, :]
    sink_b = jnp.broadcast_to(od_sink[0][:, None], (WIN_KV * WIN_GROUP, 128))
    final_g = final_norm_g[None, :]

    xc, ssd_f, ssd_b = even_layer(x_prompt, mods[0][0], None, None, True)
    tm = _row_tile(xc.shape[1])
    qc, kc, vc, zc, qd, kd, vd, zd = _odd_inproj(xc, mods[1][0], w_odd, tm, None)
    oc = _diff_attn(qc, kc, vc, None, None, zc, lam_p, subln_g, xc.shape[1], lam_init, hps=DIFF_HEADS)
    od = _sink_attn(qd, kd, vd, None, None, zd, sink_b, False)
    y_prompt = _odd_out(oc, od, xc, mods[1][0], w_out_odd, final_g, tm)

    xl, _, _ = even_layer(x_sample, mods[0][1], state_ssd_fwd[:, 0], state_ssd_bwd[:, 0], False)
    t_lat = xl.shape[1]
    tm = _row_tile(t_lat)
    lq, lk, lv, lz, lqd, lkd, lvd, lzd = _odd_inproj(xl, mods[1][1], w_odd, tm, _rope_tables(t_lat))
    loc = _diff_attn(lq, lk, lv, cache_diff_k[:, 0].astype(BF16), cache_diff_v[:, 0].astype(BF16), lz, lam_p,
                     subln_g, min(1024, t_lat), lam_init)
    lod = _sink_attn(lqd, lkd, lvd, cache_win_k[:, 0].astype(BF16), cache_win_v[:, 0].astype(BF16), lzd, sink_b,
                     True)
    y_sample = _odd_out(loc, lod, xl, mods[1][1], w_out_odd, final_g, tm)

    return (y_prompt, y_sample, ssd_f[:, None], ssd_b[:, None], kc[:, None], vc[:, None], kd[:, None],
            vd[:, None])
```

```python
import functools
import math

import jax
import jax.numpy as jnp
import numpy as np
from jax import lax
from jax.experimental import pallas as pl
from jax.experimental.pallas import tpu as pltpu

F32 = jnp.float32
BF16 = jnp.bfloat16
HIGHEST = lax.Precision.HIGHEST

D = 1024
EPS = 1e-6
NEG = -1e30
LOG2E = 1.4426950408889634
GRID_W = 64
ROPE_THETA = 10000.0
Q = 128
SSD_HEADS = 16
SSD_P = 64
SSD_N = 128
SSD_GROUPS = 4
D_CONV = 5
CONV_CH = 2048
POOL_WINDOWS = (2, 4, 8, 16)
POOL_GROUP = 256
HALO = 8
DIFF_HEADS = 8
DIFF_HD = 64
WIN_KV = 4
WIN_GROUP = 4
WIN_HD = 64
VMEM_LIMIT = 56 * 1024 * 1024


def _cparams(*sem):
    return pltpu.CompilerParams(dimension_semantics=sem, vmem_limit_bytes=VMEM_LIMIT)


def _dot(a, b, **kw):
    return jnp.dot(a, b, preferred_element_type=F32, **kw)


def _dot_nt(a, b):
    return lax.dot_general(a, b, (((1,), (1,)), ((), ())), preferred_element_type=F32)


def _silu(x):
    return x * jax.nn.sigmoid(x)


def _softplus(x):
    return jnp.maximum(x, 0.0) + jnp.log1p(jnp.exp(-jnp.abs(x)))


def _modnorm(x, mod):
    shift = mod[:, :D]
    scale = mod[:, D:2 * D]
    ms = jnp.mean(x * x, axis=-1, keepdims=True)
    return x * lax.rsqrt(ms + EPS) * (1.0 + scale) + shift


def _const_spec(shape):
    nd = len(shape)
    return pl.BlockSpec(shape, lambda *_: (0,) * nd)


def _ada_kernel(c_ref, w_ref, b_ref, o_ref):
    o_ref[0] = _dot(_silu(c_ref[...]), w_ref[0], precision=HIGHEST) + b_ref[0]


def _ada(cvec, w_ada, b_ada):
    depth, _, n3 = w_ada.shape
    rows = cvec.shape[0]
    tn = 1024
    return pl.pallas_call(
        _ada_kernel,
        grid=(depth, n3 // tn),
        in_specs=[pl.BlockSpec((rows, D), lambda l, j: (0, 0)),
                  pl.BlockSpec((1, D, tn), lambda l, j: (l, 0, j)),
                  pl.BlockSpec((1, 1, tn), lambda l, j: (l, 0, j))],
        out_specs=pl.BlockSpec((1, rows, tn), lambda l, j: (l, 0, j)),
        out_shape=jax.ShapeDtypeStruct((depth, rows, n3), F32),
        compiler_params=_cparams("arbitrary", "arbitrary"),
    )(cvec, w_ada, b_ada.reshape(depth, 1, n3))


EVEN_SPLITS = (1024, 2048, 1024, 1024, 128)


def _even_inproj_kernel(x_ref, xp_ref, xn_ref, mod_ref, w_ref, cw_ref, cb_ref,
                        za_ref, act_ref, zb_ref, xb_ref, dt_ref, ext_ref, *, nt):
    i = pl.program_id(1)
    mod = mod_ref[0]
    tm = x_ref.shape[1]
    hb = _modnorm(x_ref[0], mod).astype(BF16)
    off = 0
    for o_ref, wd in zip((za_ref, None, zb_ref, xb_ref, dt_ref), EVEN_SPLITS):
        if o_ref is not None:
            o_ref[0] = _dot(hb, w_ref[:, off:off + wd])
        off += wd
    x_ext = jnp.concatenate([xp_ref[0], x_ref[0], xn_ref[0]], axis=0)
    h_ext = _modnorm(x_ext, mod).astype(BF16)
    u = _dot(h_ext, w_ref[:, EVEN_SPLITS[0]:EVEN_SPLITS[0] + CONV_CH])
    row = lax.broadcasted_iota(jnp.int32, (tm + 2 * HALO, 1), 0)
    keep = ((row >= HALO) | (i > 0)) & ((row < HALO + tm) | (i < nt - 1))
    ext_ref[...] = jnp.where(keep, u, 0.0)
    pad = D_CONV // 2
    acc = None
    for k in range(D_CONV):
        term = ext_ref[HALO - pad + k:HALO - pad + k + tm, :] * cw_ref[k:k + 1, :]
        acc = term if acc is None else acc + term
    act_ref[0] = _silu(acc + cb_ref[...])


def _even_inproj(x, mod, w, conv_w, conv_b, tm):
    b, t, _ = x.shape
    nt = t // tm
    rpt = tm // HALO
    nblk = t // HALO
    nmod = mod.shape[0]
    ntot = w.shape[1]
    mod_map = (lambda bi, i: (bi, 0, 0)) if nmod > 1 else (lambda bi, i: (0, 0, 0))
    return pl.pallas_call(
        functools.partial(_even_inproj_kernel, nt=nt),
        grid=(b, nt),
        in_specs=[pl.BlockSpec((1, tm, D), lambda bi, i: (bi, i, 0)),
                  pl.BlockSpec((1, HALO, D), lambda bi, i: (bi, jnp.maximum(i * rpt - 1, 0), 0)),
                  pl.BlockSpec((1, HALO, D), lambda bi, i: (bi, jnp.minimum((i + 1) * rpt, nblk - 1), 0)),
                  pl.BlockSpec((1, 1, 3 * D), mod_map),
                  _const_spec((D, ntot)), _const_spec((D_CONV, CONV_CH)), _const_spec((1, CONV_CH))],
        out_specs=[pl.BlockSpec((1, tm, wd), lambda bi, i: (bi, i, 0)) for wd in EVEN_SPLITS],
        out_shape=[jax.ShapeDtypeStruct((b, t, wd), F32) for wd in EVEN_SPLITS],
        scratch_shapes=[pltpu.VMEM((tm + 2 * HALO, CONV_CH), F32)],
        compiler_params=_cparams("parallel", "parallel"),
    )(x, x, x, mod, w, conv_w, conv_b)


def _fill_halo(ext_ref, main, prev, nxt, has_prev, has_next, rows):
    ext_ref[0:HALO] = jnp.where(has_prev, prev, 0.0)
    ext_ref[HALO:HALO + rows] = main
    ext_ref[HALO + rows:2 * HALO + rows] = jnp.where(has_next, nxt, 0.0)


SSD_CPS = 4


def _tri_masks():
    row = lax.broadcasted_iota(jnp.int32, (Q, Q), 0)
    col = lax.broadcasted_iota(jnp.int32, (Q, Q), 1)
    return row >= col, col >= row


def _head_expand(vec, lane_base):
    k = lax.broadcasted_iota(jnp.int32, (SSD_N, SSD_HEADS * SSD_P), 0)
    j = lax.broadcasted_iota(jnp.int32, (SSD_N, SSD_HEADS * SSD_P), 1)
    sel = (k == lane_base + j // SSD_P).astype(F32)
    return _dot(jnp.broadcast_to(vec, (8, SSD_N)), sel, precision=HIGHEST)[0:1]


def _expand_cols(arr, sel):
    hi = arr.astype(BF16)
    lo = (arr - hi.astype(F32)).astype(BF16)
    return _dot(hi, sel) + _dot(lo, sel)


def _load_state_t(h0_ref, ht_ref):
    for j in range(SSD_HEADS // 2):
        pair = jnp.concatenate([h0_ref[0, 2 * j], h0_ref[0, 2 * j + 1]], axis=0)
        ht_ref[:, j * 128:(j + 1) * 128] = pair.T


def _store_state(ht_ref, out_ref):
    for j in range(SSD_HEADS // 2):
        tr = ht_ref[:, j * 128:(j + 1) * 128].T
        out_ref[0, 2 * j] = tr[0:SSD_P]
        out_ref[0, 2 * j + 1] = tr[SSD_P:2 * SSD_P]


def _ssd_bwd_kernel(*refs, nc, cps, has_h0, want_final):
    it = iter(refs)
    act_ref, dt_ref, prm_ref, sel_ref = (next(it) for _ in range(4))
    h0_ref = next(it) if has_h0 else None
    hprev_ref = next(it)
    hfin_ref = next(it) if want_final else None
    ht_ref = next(it)

    i = pl.program_id(1)

    @pl.when(i == 0)
    def _():
        if has_h0:
            _load_state_t(h0_ref, ht_ref)
        else:
            ht_ref[...] = jnp.zeros_like(ht_ref)

    _, upper = _tri_masks()
    for cc in reversed(range(cps)):
        rows = slice(cc * Q, (cc + 1) * Q)
        act = act_ref[0, rows, :]

        dt = _softplus(dt_ref[0, rows, :] + prm_ref[0:1, :])
        a = dt * (-jnp.exp(prm_ref[1:2, :]))
        acs = _dot(upper.astype(F32), a, precision=HIGHEST)
        wgt = dt * jnp.exp(acs[0:1, :] - acs)
        cd_exp = _head_expand(jnp.exp(acs[0:1, :]), SSD_HEADS)

        wgt_x = _expand_cols(wgt, sel_ref[1])
        hprev_ref[0, cc] = ht_ref[...]
        for g in range(SSD_GROUPS):
            xw = (act[:, g * 256:(g + 1) * 256] * wgt_x[:, g * 256:(g + 1) * 256]).astype(BF16)
            bt = act[:, D + g * SSD_N:D + (g + 1) * SSD_N].T.astype(BF16)
            sl = slice(g * 256, (g + 1) * 256)
            ht_ref[:, sl] = ht_ref[:, sl] * cd_exp[:, sl] + _dot(bt, xw)

    if want_final:
        @pl.when(i == nc // cps - 1)
        def _():
            _store_state(ht_ref, hfin_ref)


def _ssd_main_kernel(*refs, nc, cps, has_h0, want_final):
    it = iter(refs)
    act_ref, dt_ref, za_ref, hpb_ref, prm_ref, prmt_ref, dsk_ref, ng_ref, sel_ref = (next(it) for _ in range(9))
    h0_ref = next(it) if has_h0 else None
    y_ref = next(it)
    hfin_ref = next(it) if want_final else None
    ht_ref, ybuf_ref = next(it), next(it)

    c = pl.program_id(1)

    @pl.when(c == 0)
    def _():
        if has_h0:
            _load_state_t(h0_ref, ht_ref)
        else:
            ht_ref[...] = jnp.zeros_like(ht_ref)

    lower, upper = _tri_masks()
    lower_f, upper_f = lower.astype(F32), upper.astype(F32)
    lo = lax.broadcasted_iota(jnp.int32, (Q, 128), 1) < SSD_P
    for cc in range(cps):
        rows = slice(cc * Q, (cc + 1) * Q)
        act = act_ref[0, rows, :]

        dtr = dt_ref[0, rows, :]
        dt_c = _softplus(dtr + prm_ref[0:1, :])
        a_c = dt_c * (-jnp.exp(prm_ref[1:2, :]))
        acsf_c = _dot(lower_f, a_c, precision=HIGHEST)
        acsb_c = _dot(upper_f, a_c, precision=HIGHEST)
        ef_c = jnp.exp(acsf_c)
        eb_c = jnp.exp(acsb_c)
        wf_c = dt_c * jnp.exp(acsf_c[Q - 1:Q, :] - acsf_c)
        cdf_exp = _head_expand(jnp.exp(acsf_c[Q - 1:Q, :]), 0)
        ef_x = _expand_cols(ef_c, sel_ref[0])
        eb_x = _expand_cols(eb_c, sel_ref[1])
        wf_x = _expand_cols(wf_c, sel_ref[0])

        nh2 = 2 * SSD_HEADS
        dt_r = _softplus(dtr.T[0:nh2] + prmt_ref[0])
        a_r = dt_r * (-jnp.exp(prmt_ref[1]))
        acsf_r = _dot(a_r, upper_f, precision=HIGHEST)
        acsb_r = _dot(a_r, lower_f, precision=HIGHEST)

        lgf_c, lgb_c = acsf_c * LOG2E, acsb_c * LOG2E
        lg_dt = jnp.log2(dt_r)
        lgf_r, lgb_r = acsf_r * LOG2E - lg_dt, acsb_r * LOG2E - lg_dt

        hpb = hpb_ref[0, cc]
        for g in range(SSD_GROUPS):
            sl = slice(g * 256, (g + 1) * 256)
            cg = act[:, D + 512 + g * SSD_N:D + 512 + (g + 1) * SSD_N].astype(BF16)
            bt = act[:, D + g * SSD_N:D + (g + 1) * SSD_N].T.astype(BF16)
            cbm = _dot(cg, bt)
            hf_old = ht_ref[:, sl]
            yof = _dot(cg, hf_old.astype(BF16))
            yob = _dot(cg, hpb[:, sl].astype(BF16))
            xw = []
            for jj in range(2):
                j = 2 * g + jj
                xs_pair = act[:, j * 128:(j + 1) * 128]
                xs_pair_b = xs_pair.astype(BF16)
                yd = []
                for hh in (2 * j, 2 * j + 1):
                    hb_ = SSD_HEADS + hh
                    lf = jnp.exp2(jnp.where(lower, lgf_c[:, hh:hh + 1] - lgf_r[hh:hh + 1, :], NEG))
                    lb = jnp.exp2(jnp.where(upper, lgb_c[:, hb_:hb_ + 1] - lgb_r[hb_:hb_ + 1, :], NEG))
                    wmat = cbm * (lf + lb)
                    yd.append(_dot(wmat.astype(BF16), xs_pair_b))
                pc = slice(j * 128, (j + 1) * 128)
                y_pair = (jnp.where(lo, yd[0], yd[1])
                          + yof[:, jj * 128:(jj + 1) * 128] * ef_x[:, pc]
                          + yob[:, jj * 128:(jj + 1) * 128] * eb_x[:, pc]
                          + xs_pair * dsk_ref[:, pc])
                ybuf_ref[cc, :, pc] = y_pair
                xw.append((xs_pair * wf_x[:, pc]).astype(BF16))
            ht_ref[:, sl] = hf_old * cdf_exp[:, sl] + _dot(bt, jnp.concatenate(xw, axis=1))

        yz = ybuf_ref[cc] * _silu(za_ref[0, rows, :])
        ms = jnp.mean(yz * yz, axis=-1, keepdims=True)
        y_ref[0, rows, :] = (yz * lax.rsqrt(ms + EPS) * ng_ref[...]).astype(y_ref.dtype)

    if want_final:
        @pl.when(c == nc // cps - 1)
        def _():
            _store_state(ht_ref, hfin_ref)


def _ssd(act, dt, za, prm, prmt, dsk, ng, h0f, h0b, want_final):
    b, t, _ = act.shape
    nc = t // Q
    has_h0 = h0f is not None
    hw = SSD_HEADS * SSD_P
    state_shape = jax.ShapeDtypeStruct((b, SSD_HEADS, SSD_P, SSD_N), F32)
    state_spec = pl.BlockSpec((1, SSD_HEADS, SSD_P, SSD_N), lambda bi, i: (bi, 0, 0, 0))
    wb = D + SSD_GROUPS * SSD_N
    lane_head = np.arange(hw) // SSD_P
    sel = jnp.asarray(np.stack([(np.arange(SSD_N)[:, None] == lane_head[None, :] + d * SSD_HEADS)
                                for d in range(2)]).astype(np.float32), dtype=BF16)

    cps = next(n for n in (SSD_CPS, 2, 1) if nc % n == 0)
    ns = nc // cps
    in_specs = [pl.BlockSpec((1, cps * Q, wb), lambda bi, i: (bi, ns - 1 - i, 0)),
                pl.BlockSpec((1, cps * Q, 128), lambda bi, i: (bi, ns - 1 - i, 0)),
                _const_spec((8, 128)), _const_spec((2, SSD_N, hw))]
    args = [act, dt, prm, sel]
    if has_h0:
        in_specs.append(state_spec)
        args.append(h0b)
    out_specs = [pl.BlockSpec((1, cps, SSD_N, hw), lambda bi, i: (bi, ns - 1 - i, 0, 0))]
    out_shape = [jax.ShapeDtypeStruct((b, nc, SSD_N, hw), F32)]
    if want_final:
        out_specs.append(state_spec)
        out_shape.append(state_shape)
    res = pl.pallas_call(
        functools.partial(_ssd_bwd_kernel, nc=nc, cps=cps, has_h0=has_h0, want_final=want_final),
        grid=(b, ns), in_specs=in_specs, out_specs=out_specs, out_shape=out_shape,
        scratch_shapes=[pltpu.VMEM((SSD_N, hw), F32)],
        compiler_params=_cparams("parallel", "arbitrary"),
    )(*args)
    hprev_b = res[0]
    hfin_b = res[1] if want_final else None

    in_specs = [pl.BlockSpec((1, cps * Q, CONV_CH), lambda bi, i: (bi, i, 0)),
                pl.BlockSpec((1, cps * Q, 128), lambda bi, i: (bi, i, 0)),
                pl.BlockSpec((1, cps * Q, hw), lambda bi, i: (bi, i, 0)),
                pl.BlockSpec((1, cps, SSD_N, hw), lambda bi, i: (bi, i, 0, 0)),
                _const_spec((8, 128)), _const_spec((2, 2 * SSD_HEADS, 128)), _const_spec((1, hw)),
                _const_spec((1, hw)), _const_spec((2, SSD_N, hw))]
    args = [act, dt, za, hprev_b, prm, prmt, dsk, ng, sel]
    if has_h0:
        in_specs.append(state_spec)
        args.append(h0f)
    out_specs = [pl.BlockSpec((1, cps * Q, hw), lambda bi, i: (bi, i, 0))]
    out_shape = [jax.ShapeDtypeStruct((b, t, hw), BF16)]
    if want_final:
        out_specs.append(state_spec)
        out_shape.append(state_shape)
    res = pl.pallas_call(
        functools.partial(_ssd_main_kernel, nc=nc, cps=cps, has_h0=has_h0, want_final=want_final),
        grid=(b, ns), in_specs=in_specs, out_specs=out_specs, out_shape=out_shape,
        scratch_shapes=[pltpu.VMEM((SSD_N, hw), F32), pltpu.VMEM((cps, Q, hw), F32)],
        compiler_params=_cparams("parallel", "arbitrary"),
    )(*args)
    return res[0], (res[1] if want_final else None), hfin_b


def _even_out_kernel(y_ref, xm_ref, xp_ref, xn_ref, zb_ref, x_ref, mod_ref, pw_ref, ps_ref, wo_ref, o_ref,
                     ext_ref, *, t_len, tm, nt):
    i = pl.program_id(1)
    _fill_halo(ext_ref, xm_ref[0], xp_ref[0], xn_ref[0], i > 0, i < nt - 1, tm)
    tpos = i * tm + lax.broadcasted_iota(jnp.int32, (tm, POOL_GROUP), 0)
    parts = []
    for g, w in enumerate(POOL_WINDOWS):
        left = w // 2
        right = w - 1 - left
        cols = slice(g * POOL_GROUP, (g + 1) * POOL_GROUP)
        s = None
        for k in range(-left, right + 1):
            v = ext_ref[HALO + k:HALO + k + tm, cols]
            s = v if s is None else s + v
        cnt = (jnp.minimum(tpos + right + 1, t_len) - jnp.maximum(tpos - left, 0)).astype(F32)
        pooled = s / cnt - ext_ref[HALO:HALO + tm, cols]
        parts.append(_dot(pooled.astype(BF16), pw_ref[g]))
    yp = jnp.concatenate(parts, axis=1) * ps_ref[...] * _silu(zb_ref[0])
    out = _dot(y_ref[0], wo_ref[0:D, :]) + _dot(yp.astype(BF16), wo_ref[D:2 * D, :])
    o_ref[0] = x_ref[0] + mod_ref[0][:, 2 * D:3 * D] * out


def _even_out(y, xb, zb, x, mod, pool_w, pool_scale, w_out, tm):
    b, t, _ = x.shape
    nt = t // tm
    rpt = tm // HALO
    nblk = t // HALO
    nmod = mod.shape[0]
    mod_map = (lambda bi, i: (bi, 0, 0)) if nmod > 1 else (lambda bi, i: (0, 0, 0))
    tile = pl.BlockSpec((1, tm, D), lambda bi, i: (bi, i, 0))
    return pl.pallas_call(
        functools.partial(_even_out_kernel, t_len=t, tm=tm, nt=nt),
        grid=(b, nt),
        in_specs=[tile, tile,
                  pl.BlockSpec((1, HALO, D), lambda bi, i: (bi, jnp.maximum(i * rpt - 1, 0), 0)),
                  pl.BlockSpec((1, HALO, D), lambda bi, i: (bi, jnp.minimum((i + 1) * rpt, nblk - 1), 0)),
                  tile, tile, pl.BlockSpec((1, 1, 3 * D), mod_map),
                  _const_spec(pool_w.shape), _const_spec((1, D)), _const_spec((2 * D, D))],
        out_specs=tile,
        out_shape=jax.ShapeDtypeStruct((b, t, D), F32),
        scratch_shapes=[pltpu.VMEM((tm + 2 * HALO, D), F32)],
        compiler_params=_cparams("parallel", "parallel"),
    )(y, xb, xb, xb, zb, x, mod, pool_w, pool_scale, w_out)


W_DIFF = DIFF_HEADS * 2 * DIFF_HD
W_WIN = WIN_KV * WIN_GROUP * WIN_HD
W_KVW = WIN_KV * WIN_HD
Q_SCALE = DIFF_HD ** -0.5 * LOG2E


def _rope(u, cos, sin, first):
    outs = []
    for j in range(u.shape[1] // 128):
        x = u[:, j * 128:(j + 1) * 128]
        partner = jnp.where(first, pltpu.roll(x, 112, 1), pltpu.roll(x, 16, 1))
        outs.append(x * cos + partner * sin)
    return outs[0] if len(outs) == 1 else jnp.concatenate(outs, axis=1)


def _odd_inproj_kernel(*refs, latent):
    it = iter(refs)
    x_ref, mod_ref, w_ref = next(it), next(it), next(it)
    if latent:
        cos_ref, sin_ref = next(it), next(it)
    qc_ref, kc_ref, vc_ref, zc_ref, qd_ref, kd_ref, vd_ref, zd_ref = (next(it) for _ in range(8))

    hb = _modnorm(x_ref[0], mod_ref[0]).astype(BF16)
    if latent:
        cos, sin = cos_ref[...], sin_ref[...]
        first = (lax.broadcasted_iota(jnp.int32, cos.shape, 1) % 32) < 16
        rope = lambda u: _rope(u, cos, sin, first)
    else:
        rope = lambda u: u

    def proj(off, width):
        return _dot(hb, w_ref[:, off:off + width])

    qc_ref[0] = (rope(proj(0, W_DIFF)) * Q_SCALE).astype(qc_ref.dtype)
    kc = rope(proj(W_DIFF, W_DIFF))
    vc = proj(2 * W_DIFF, W_DIFF)
    if latent:
        kc_ref[0] = kc.astype(kc_ref.dtype)
        vc_ref[0] = vc.astype(vc_ref.dtype)
    else:
        for h in range(DIFF_HEADS):
            kc_ref[0, h] = kc[:, h * 128:(h + 1) * 128]
            vc_ref[0, h] = vc[:, h * 128:(h + 1) * 128]
    zc_ref[0] = proj(3 * W_DIFF, W_DIFF)
    off = 4 * W_DIFF
    qd_ref[0] = (rope(proj(off, W_WIN)) * Q_SCALE).astype(qd_ref.dtype)
    kd = rope(proj(off + W_WIN, W_KVW))
    vd = proj(off + W_WIN + W_KVW, W_KVW)
    for kv in range(WIN_KV):
        kd_ref[0, kv] = kd[:, kv * WIN_HD:(kv + 1) * WIN_HD].astype(kd_ref.dtype)
        vd_ref[0, kv] = vd[:, kv * WIN_HD:(kv + 1) * WIN_HD].astype(vd_ref.dtype)
    zd_ref[0] = proj(off + W_WIN + 2 * W_KVW, W_WIN)


def _odd_inproj(x, mod, w, tm, rope_tabs):
    b, t, _ = x.shape
    latent = rope_tabs is not None
    nmod = mod.shape[0]
    mod_map = (lambda bi, i: (bi, 0, 0)) if nmod > 1 else (lambda bi, i: (0, 0, 0))
    tile = lambda wd: pl.BlockSpec((1, tm, wd), lambda bi, i: (bi, i, 0))
    hm = lambda nh, hd: pl.BlockSpec((1, nh, tm, hd), lambda bi, i: (bi, 0, i, 0))
    in_specs = [tile(D), pl.BlockSpec((1, 1, 3 * D), mod_map), _const_spec(w.shape)]
    args = [x, mod, w]
    if latent:
        in_specs += [pl.BlockSpec((tm, 128), lambda bi, i: (i, 0))] * 2
        args += list(rope_tabs)
        kv_specs = [tile(W_DIFF), tile(W_DIFF)]
        kv_shapes = [jax.ShapeDtypeStruct((b, t, W_DIFF), BF16)] * 2
        wdt = BF16
    else:
        kv_specs = [hm(DIFF_HEADS, 128), hm(DIFF_HEADS, 128)]
        kv_shapes = [jax.ShapeDtypeStruct((b, DIFF_HEADS, t, 128), F32)] * 2
        wdt = F32
    out_specs = ([tile(W_DIFF)] + kv_specs + [tile(W_DIFF), tile(W_WIN), hm(WIN_KV, WIN_HD), hm(WIN_KV, WIN_HD),
                                              tile(W_WIN)])
    out_shape = ([jax.ShapeDtypeStruct((b, t, W_DIFF), BF16)] + kv_shapes
                 + [jax.ShapeDtypeStruct((b, t, W_DIFF), F32), jax.ShapeDtypeStruct((b, t, W_WIN), BF16),
                    jax.ShapeDtypeStruct((b, WIN_KV, t, WIN_HD), wdt),
                    jax.ShapeDtypeStruct((b, WIN_KV, t, WIN_HD), wdt),
                    jax.ShapeDtypeStruct((b, t, W_WIN), F32)])
    return pl.pallas_call(
        functools.partial(_odd_inproj_kernel, latent=latent),
        grid=(b, t // tm), in_specs=in_specs, out_specs=out_specs, out_shape=out_shape,
        compiler_params=_cparams("parallel", "parallel"),
    )(*args)


DIFF_SUB = 256


def _diff_attn_kernel(*refs, n_ctx, head_major, lam_init, hps):
    it = iter(refs)
    q_ref, kl_ref, vl_ref = next(it), next(it), next(it)
    if n_ctx:
        kc_ref, vc_ref = next(it), next(it)
    z_ref, lam_ref, g_ref, o_ref, kall_ref, vext_ref = (next(it) for _ in range(6))

    @pl.when(pl.program_id(2) == 0)
    def _():
        for hh in range(hps):
            kl = kl_ref[0, hh] if head_major else kl_ref[0]
            vl = vl_ref[0, hh] if head_major else vl_ref[0]
            if n_ctx:
                kall_ref[hh, 0:n_ctx] = kc_ref[0, hh].astype(BF16)
                vext_ref[hh, 0:n_ctx, 0:128] = vc_ref[0, hh].astype(BF16)
            kall_ref[hh, n_ctx:] = kl.astype(BF16)
            vext_ref[hh, n_ctx:, 0:128] = vl.astype(BF16)
            vext_ref[hh, :, 128:256] = jnp.ones((vext_ref.shape[1], 128), BF16)

    lp = lam_ref[...]
    lam = (jnp.exp(jnp.sum(lp[0:1] * lp[1:2], axis=-1, keepdims=True))
           - jnp.exp(jnp.sum(lp[2:3] * lp[3:4], axis=-1, keepdims=True)) + lam_init)
    ts = DIFF_SUB
    lo = lax.broadcasted_iota(jnp.int32, (ts, 128), 1) < DIFF_HD
    zero = jnp.zeros((ts, 128), q_ref.dtype)
    for hh in range(hps):
        cols = slice(hh * 128, (hh + 1) * 128)
        for a in range(q_ref.shape[1] // ts):
            rows = slice(a * ts, (a + 1) * ts)
            q = q_ref[0, rows, cols]
            q2 = jnp.concatenate([jnp.where(lo, q, zero), jnp.where(lo, zero, q)], axis=0)
            s = _dot_nt(q2, kall_ref[hh])
            m = jnp.max(s, axis=-1, keepdims=True)
            p = jnp.exp2(s - m).astype(BF16)
            oe = _dot(p, vext_ref[hh])
            on2 = oe[:, 0:128] / oe[:, 128:129]
            o = on2[0:ts] - lam * on2[ts:2 * ts]
            ms = jnp.mean(o * o, axis=-1, keepdims=True)
            on = o * lax.rsqrt(ms + EPS) * g_ref[...] * (1.0 - lam_init)
            o_ref[0, rows, cols] = (on * _silu(z_ref[0, rows, cols])).astype(o_ref.dtype)


def _diff_attn(q, k, v, kc, vc, z, lam_p, subln_g, tq, lam_init, hps=1):
    b, t, _ = q.shape
    head_major = k.ndim == 4
    assert head_major or hps == 1
    n_ctx = 0 if kc is None else kc.shape[2]
    nk = n_ctx + t
    qspec = pl.BlockSpec((1, tq, hps * 128), lambda bi, h, i: (bi, i, h))
    if head_major:
        kvspec = pl.BlockSpec((1, hps, t, 128), lambda bi, h, i: (bi, h, 0, 0))
    else:
        kvspec = pl.BlockSpec((1, t, 128), lambda bi, h, i: (bi, 0, h))
    in_specs = [qspec, kvspec, kvspec]
    args = [q, k, v]
    if n_ctx:
        cspec = pl.BlockSpec((1, hps, n_ctx, 128), lambda bi, h, i: (bi, h, 0, 0))
        in_specs += [cspec, cspec]
        args += [kc, vc]
    in_specs += [qspec, _const_spec((4, DIFF_HD)), _const_spec((1, 128))]
    args += [z, lam_p, subln_g]
    return pl.pallas_call(
        functools.partial(_diff_attn_kernel, n_ctx=n_ctx, head_major=head_major, lam_init=lam_init, hps=hps),
        grid=(b, DIFF_HEADS // hps, t // tq), in_specs=in_specs, out_specs=qspec,
        out_shape=jax.ShapeDtypeStruct((b, t, W_DIFF), BF16),
        scratch_shapes=[pltpu.VMEM((hps, nk, 128), BF16), pltpu.VMEM((hps, nk, 256), BF16)],
        compiler_params=_cparams("parallel", "parallel", "arbitrary"),
    )(*args)


WIN_SUB = 2


def _sink_attn_kernel(*refs, n_ctx, window, nb, rows, nq, kps):
    it = iter(refs)
    q_ref = next(it)
    if window:
        kp_ref, kcur_ref, kn_ref, vp_ref, vcur_ref, vn_ref = (next(it) for _ in range(6))
    else:
        kcur_ref, vcur_ref = next(it), next(it)
    if n_ctx:
        kc_ref, vc_ref = next(it), next(it)
    z_ref, sink_ref, o_ref = next(it), next(it), next(it)

    i = pl.program_id(2)
    srows = WIN_GROUP * rows
    gw = WIN_GROUP * WIN_HD
    if window:
        r = lax.broadcasted_iota(jnp.int32, (srows, 3 * Q), 0) % rows
        j = lax.broadcasted_iota(jnp.int32, (srows, 3 * Q), 1)
        band = jnp.where((j >= r) & (j <= r + 2 * Q), 0.0, NEG)
        jrow = lax.broadcasted_iota(jnp.int32, (1, 3 * Q), 1)
    for kk in range(kps):
        kvh = pl.program_id(1) * kps + kk
        if window:
            kwin = jnp.concatenate([kp_ref[0, kk], kcur_ref[0, kk], kn_ref[0, kk]], axis=0).astype(BF16)
            vwin = jnp.concatenate([vp_ref[0, kk], vcur_ref[0, kk], vn_ref[0, kk]], axis=0).astype(BF16)
        else:
            kwin = kcur_ref[0, kk].astype(BF16)
            vwin = vcur_ref[0, kk].astype(BF16)
        if n_ctx:
            kc = kc_ref[0, kk].astype(BF16)
            vc = vc_ref[0, kk].astype(BF16)
        sk = jnp.concatenate(
            [jnp.broadcast_to(sink_ref[pl.ds(kvh * WIN_GROUP + g, 1), :], (rows, 128)) for g in range(WIN_GROUP)],
            axis=0)[:, 0:1] * LOG2E
        for a in range(nq):
            rs = slice(a * rows, (a + 1) * rows)
            qb = q_ref[0, rs, kk * gw:(kk + 1) * gw]
            q4 = jnp.concatenate([qb[:, g * WIN_HD:(g + 1) * WIN_HD] for g in range(WIN_GROUP)], axis=0)
            if window:
                gb = i * nq + a
                kw = kwin[a * Q:(a + 3) * Q]
                vw = vwin[a * Q:(a + 3) * Q]
                edge = jnp.where(((jrow < Q) & (gb == 0)) | ((jrow >= 2 * Q) & (gb == nb - 1)), NEG, 0.0)
                sw = _dot_nt(q4, kw) + (band + edge)
            else:
                kw, vw = kwin, vwin
                sw = _dot_nt(q4, kw)
            m = jnp.maximum(jnp.max(sw, axis=-1, keepdims=True), sk)
            if n_ctx:
                sc = _dot_nt(q4, kc)
                m = jnp.maximum(m, jnp.max(sc, axis=-1, keepdims=True))
            pw = jnp.exp2(sw - m)
            den = jnp.sum(pw, axis=-1, keepdims=True) + jnp.exp2(sk - m)
            acc = _dot(pw.astype(BF16), vw)
            if n_ctx:
                pc = jnp.exp2(sc - m)
                den = den + jnp.sum(pc, axis=-1, keepdims=True)
                acc = acc + _dot(pc.astype(BF16), vc)
            o4 = acc / den
            ob = jnp.concatenate([o4[g * rows:(g + 1) * rows] for g in range(WIN_GROUP)], axis=1)
            o_ref[0, rs, kk * gw:(kk + 1) * gw] = (ob * _silu(z_ref[0, rs, kk * gw:(kk + 1) * gw])).astype(o_ref.dtype)


def _sink_attn(q, k, v, kc, vc, z, sink_b, window):
    b, t, _ = q.shape
    n_ctx = 0 if kc is None else kc.shape[2]
    gw = WIN_GROUP * WIN_HD
    if window:
        nq, rows, nb = WIN_SUB, Q, t // Q
        nsteps = nb // nq
        blk = lambda n, f: pl.BlockSpec((1, 1, n * Q, WIN_HD), lambda bi, kv, i: (bi, kv, f(i), 0))
        kv_specs = [blk(1, lambda i: jnp.maximum(i * nq - 1, 0)), blk(nq, lambda i: i),
                    blk(1, lambda i: jnp.minimum((i + 1) * nq, nb - 1))]
        in_kv = kv_specs + kv_specs
        args_kv = [k, k, k, v, v, v]
        kps = 1
    else:
        nq, rows, nb, nsteps, kps = 1, t, 1, 1, WIN_KV
        in_kv = [pl.BlockSpec((1, kps, t, WIN_HD), lambda bi, kv, i: (bi, kv, 0, 0))] * 2
        args_kv = [k, v]
    qspec = pl.BlockSpec((1, nq * rows, kps * gw), lambda bi, kv, i: (bi, i, kv))
    in_specs = [qspec] + in_kv
    args = [q] + args_kv
    if n_ctx:
        cspec = pl.BlockSpec((1, 1, n_ctx, WIN_HD), lambda bi, kv, i: (bi, kv, 0, 0))
        in_specs += [cspec, cspec]
        args += [kc, vc]
    in_specs += [qspec, _const_spec(sink_b.shape)]
    args += [z, sink_b]
    return pl.pallas_call(
        functools.partial(_sink_attn_kernel, n_ctx=n_ctx, window=window, nb=nb, rows=rows, nq=nq, kps=kps),
        grid=(b, WIN_KV // kps, nsteps), in_specs=in_specs, out_specs=qspec,
        out_shape=jax.ShapeDtypeStruct((b, t, W_WIN), BF16),
        compiler_params=_cparams("parallel", "parallel", "arbitrary"),
    )(*args)


def _odd_out_kernel(oc_ref, od_ref, x_ref, mod_ref, wo_ref, fg_ref, o_ref):
    out = _dot(oc_ref[0], wo_ref[0:W_DIFF, :]) + _dot(od_ref[0], wo_ref[W_DIFF:W_DIFF + W_WIN, :])
    x = x_ref[0] + mod_ref[0][:, 2 * D:3 * D] * out
    ms = jnp.mean(x * x, axis=-1, keepdims=True)
    o_ref[0] = x * lax.rsqrt(ms + EPS) * fg_ref[...]


def _odd_out(oc, od, x, mod, w_out, final_g, tm):
    b, t, _ = x.shape
    nmod = mod.shape[0]
    mod_map = (lambda bi, i: (bi, 0, 0)) if nmod > 1 else (lambda bi, i: (0, 0, 0))
    tile = pl.BlockSpec((1, tm, D), lambda bi, i: (bi, i, 0))
    return pl.pallas_call(
        _odd_out_kernel,
        grid=(b, t // tm),
        in_specs=[tile, tile, tile, pl.BlockSpec((1, 1, 3 * D), mod_map), _const_spec(w_out.shape),
                  _const_spec((1, D))],
        out_specs=tile,
        out_shape=jax.ShapeDtypeStruct((b, t, D), F32),
        compiler_params=_cparams("parallel", "parallel"),
    )(oc, od, x, mod, w_out, final_g)


def _rope_tables(t_len):
    nf = DIFF_HD // 4
    pos = jnp.arange(t_len)
    row = (pos // GRID_W).astype(F32)
    col = (pos % GRID_W).astype(F32)
    inv = ROPE_THETA ** (-jnp.arange(nf, dtype=F32) / nf)
    ang_r = row[:, None] * inv
    ang_c = col[:, None] * inv
    cos = jnp.concatenate([jnp.cos(ang_r)] * 2 + [jnp.cos(ang_c)] * 2, axis=1)
    sin = jnp.concatenate([-jnp.sin(ang_r), jnp.sin(ang_r), -jnp.sin(ang_c), jnp.sin(ang_c)], axis=1)
    return jnp.tile(cos, (1, 2)), jnp.tile(sin, (1, 2))


def _row_tile(t_len):
    return 512 if t_len % 512 == 0 else 256


def kernel(x_prompt, x_sample, state_ssd_fwd, state_ssd_bwd, cache_diff_k, cache_diff_v, cache_win_k, cache_win_v, c, c_ctx, w_ada, b_ada, ev_w_in, ev_conv_w, ev_conv_b, ev_A_log, ev_dt_bias, ev_D, ev_norm_g, ev_pool_w, ev_pool_scale, ev_w_out, od_w_in, od_lambda, od_subln_g, od_sink, od_w_out, final_norm_g):
    n_lat = x_sample.shape[0]
    rows = -(-(1 + n_lat) // 8) * 8
    cvec = jnp.concatenate([c_ctx[None, :], c, jnp.zeros((rows - 1 - n_lat, D), F32)], axis=0)
    ada = _ada(cvec, w_ada, b_ada)
    mods = [(ada[l, 0:1][:, None, :], ada[l, 1:1 + n_lat][:, None, :]) for l in range(2)]

    w = ev_w_in[0]
    c_dt = D + CONV_CH
    n_dt = 2 * SSD_HEADS
    w_even = jnp.concatenate([w[:, :c_dt], w[:, c_dt + n_dt:], w[:, c_dt:c_dt + n_dt],
                              jnp.zeros((D, 128 - n_dt), F32)], axis=1).astype(BF16)
    pad = jnp.zeros((128 - n_dt,), F32)
    bias_row = jnp.concatenate([ev_dt_bias[0].reshape(-1), pad])
    alog_row = jnp.concatenate([ev_A_log[0].reshape(-1), pad])
    prm = jnp.concatenate([bias_row[None], alog_row[None], jnp.zeros((6, 128), F32)], axis=0)
    prmt = jnp.stack([jnp.broadcast_to(ev_dt_bias[0].reshape(-1, 1), (n_dt, 128)),
                      jnp.broadcast_to(ev_A_log[0].reshape(-1, 1), (n_dt, 128))])
    dsk = jnp.repeat(ev_D[0], SSD_P)[None, :]
    ng = ev_norm_g[0][None, :]
    conv_w = ev_conv_w[0]
    conv_b = ev_conv_b[0][None, :]
    pool_w = ev_pool_w[0].astype(BF16)
    pool_scale = ev_pool_scale[0][None, :]
    w_out_even = ev_w_out[0].astype(BF16)

    def even_layer(x, mod, h0f, h0b, want_final):
        tm = _row_tile(x.shape[1])
        za, act, zb, xb, dt = _even_inproj(x, mod, w_even, conv_w, conv_b, tm)
        y, hf, hb = _ssd(act, dt, za, prm, prmt, dsk, ng, h0f, h0b, want_final)
        return _even_out(y, xb, zb, x, mod, pool_w, pool_scale, w_out_even, tm), hf, hb

    w_odd = od_w_in[0].astype(BF16)
    w_out_odd = od_w_out[0].astype(BF16)
    lam_init = 0.8 - 0.6 * math.exp(-0.3 * 1)
    lam_p = od_lambda[0]
    subln_g = od_subln_g[0][None, :]
    sink_b = jnp.broadcast_to(od_sink[0][:, None], (WIN_KV * WIN_GROUP, 128))
    final_g = final_norm_g[None, :]

    xc, ssd_f, ssd_b = even_layer(x_prompt, mods[0][0], None, None, True)
    tm = _row_tile(xc.shape[1])
    qc, kc, vc, zc, qd, kd, vd, zd = _odd_inproj(xc, mods[1][0], w_odd, tm, None)
    oc = _diff_attn(qc, kc, vc, None, None, zc, lam_p, subln_g, xc.shape[1], lam_init, hps=DIFF_HEADS)
    od = _sink_attn(qd, kd, vd, None, None, zd, sink_b, False)
    y_prompt = _odd_out(oc, od, xc, mods[1][0], w_out_odd, final_g, tm)

    xl, _, _ = even_layer(x_sample, mods[0][1], state_ssd_fwd[:, 0], state_ssd_bwd[:, 0], False)
    t_lat = xl.shape[1]
    tm = _row_tile(t_lat)
    lq, lk, lv, lz, lqd, lkd, lvd, lzd = _odd_inproj(xl, mods[1][1], w_odd, tm, _rope_tables(t_lat))
    loc = _diff_attn(lq, lk, lv, cache_diff_k[:, 0].astype(BF16), cache_diff_v[:, 0].astype(BF16), lz, lam_p,
                     subln_g, t_lat, lam_init)
    lod = _sink_attn(lqd, lkd, lvd, cache_win_k[:, 0].astype(BF16), cache_win_v[:, 0].astype(BF16), lzd, sink_b,
                     True)
    y_sample = _odd_out(loc, lod, xl, mods[1][1], w_out_odd, final_g, tm)

    return (y_prompt, y_sample, ssd_f[:, None], ssd_b[:, None], kc[:, None], vc[:, None], kd[:, None],
            vd[:, None])
```
